```python
import jax
import jax.numpy as jnp
from jax import lax
import numpy as np

D_MODEL = 4096
BATCH = 4
SEQ = 2048
DEPTH = 2
DEC_BATCH = 32
DEC_SEQ = 4
PAST_LEN = 16384
PAGE_SIZE = 128

N_HEADS = 32
N_KV_HEADS = 8
HEAD_DIM = D_MODEL // N_HEADS
GROUP = N_HEADS // N_KV_HEADS
Q_DIM = N_HEADS * HEAD_DIM
KV_DIM = N_KV_HEADS * HEAD_DIM
ROT_DIM = HEAD_DIM // 4
ROPE_THETA = 500000.0
WINDOW = 128
BLOCK = 128
D_FF = (D_MODEL * 7) // 2
N_EXPERTS = 8
TOP_K = 2
MOE_BLOCK = 128
N_MIXERS = 2
N_SWA_LAYERS = (DEPTH + 1) // 2
N_FOX_LAYERS = DEPTH // 2
N_DENSE_LAYERS = (DEPTH + 1) // 2
N_MOE_LAYERS = DEPTH // 2
RMS_EPS = 1e-6

kernel_name = 'hybrid_swa_sink_fox_moe_adaln_step'


def rmsnorm(x, g):
    x32 = x.astype(jnp.float32)
    y = x32 * lax.rsqrt(jnp.mean(x32 * x32, axis=-1, keepdims=True) + RMS_EPS)
    return (y * g.astype(jnp.float32)).astype(x.dtype)


def ada_params(c, w, b):
    return jnp.split(jax.nn.silu(c) @ w + b, 6, axis=-1)


def modulate(h, shift, scale):
    return h * (1.0 + scale[:, None, :]) + shift[:, None, :]


def rope_partial(x, pos):
    inv_freq = ROPE_THETA ** (-jnp.arange(0, ROT_DIM, 2, dtype=jnp.float32) / ROT_DIM)
    ang = pos.astype(jnp.float32)[:, None] * inv_freq[None, :]
    cos = jnp.cos(ang)[:, None, :]
    sin = jnp.sin(ang)[:, None, :]
    xr = x[..., :ROT_DIM].astype(jnp.float32)
    x1, x2 = xr[..., :ROT_DIM // 2], xr[..., ROT_DIM // 2:]
    rot = jnp.concatenate([x1 * cos - x2 * sin, x2 * cos + x1 * sin], axis=-1)
    return jnp.concatenate([rot.astype(x.dtype), x[..., ROT_DIM:]], axis=-1)


def swiglu(x, w_g, w_u, w_d):
    return (jax.nn.silu(x @ w_g) * (x @ w_u)) @ w_d


def sink_attend(q, k, v, mask, sinks):
    s = jnp.einsum('...qhgd,...khd->...hgqk', q, k).astype(jnp.float32) * (HEAD_DIM ** -0.5)
    s = jnp.where(mask, s, -jnp.inf)
    sink = sinks.astype(jnp.float32)[:, :, None, None]
    m = jnp.maximum(s.max(axis=-1, keepdims=True), sink)
    p = jnp.exp(s - m)
    p = (p / (p.sum(axis=-1, keepdims=True) + jnp.exp(sink - m))).astype(v.dtype)
    return jnp.einsum('...hgqk,...khd->...qhgd', p, v)


def swa_prompt(h, w_qkv, sinks, w_o):
    B, S, _ = h.shape
    q, k, v = jnp.split(h @ w_qkv, [Q_DIM, Q_DIM + KV_DIM], axis=-1)
    pos = jnp.arange(S)
    q = rope_partial(q.reshape(B, S, N_HEADS, HEAD_DIM), pos)
    k = rope_partial(k.reshape(B, S, N_KV_HEADS, HEAD_DIM), pos)
    v = v.reshape(B, S, N_KV_HEADS, HEAD_DIM)
    nb = S // BLOCK
    qb = q.reshape(B, nb, BLOCK, N_KV_HEADS, GROUP, HEAD_DIM)

    def band(t):
        t = t.reshape(B, nb, BLOCK, N_KV_HEADS, HEAD_DIM)
        prev = jnp.pad(t[:, :-1], ((0, 0), (1, 0), (0, 0), (0, 0), (0, 0)))
        return jnp.concatenate([prev, t], axis=2)

    qi = jnp.arange(BLOCK)[:, None]
    kj = jnp.arange(2 * BLOCK)[None, :]
    diff = qi + BLOCK - kj
    local = (diff >= 0) & (diff <= WINDOW)
    has_prev = (jnp.arange(nb) > 0)[:, None, None] | (kj >= BLOCK)[None]
    mask = (local[None] & has_prev)[:, None, None]
    o = sink_attend(qb, band(k), band(v), mask, sinks.reshape(N_KV_HEADS, GROUP))
    o = o.reshape(B, S, Q_DIM) @ w_o
    return o, k[:, S - WINDOW:], v[:, S - WINDOW:]


def swa_sample(h, win_k, win_v, w_qkv, sinks, w_o):
    DB, T, _ = h.shape
    q, k, v = jnp.split(h @ w_qkv, [Q_DIM, Q_DIM + KV_DIM], axis=-1)
    pos = PAST_LEN + jnp.arange(T)
    q = rope_partial(q.reshape(DB, T, N_HEADS, HEAD_DIM), pos)
    k = rope_partial(k.reshape(DB, T, N_KV_HEADS, HEAD_DIM), pos)
    v = v.reshape(DB, T, N_KV_HEADS, HEAD_DIM)
    keys = jnp.concatenate([win_k.astype(k.dtype), k], axis=1)
    vals = jnp.concatenate([win_v.astype(v.dtype), v], axis=1)
    kpos = jnp.concatenate([PAST_LEN - WINDOW + jnp.arange(WINDOW), pos])
    diff = pos[:, None] - kpos[None, :]
    mask = (diff >= 0) & (diff <= WINDOW)
    o = sink_attend(q.reshape(DB, T, N_KV_HEADS, GROUP, HEAD_DIM), keys, vals, mask,
                    sinks.reshape(N_KV_HEADS, GROUP))
    o = o.reshape(DB, T, Q_DIM) @ w_o
    return o, keys[:, T:], vals[:, T:]


def fox_project(h, w_qkvf, b_f):
    N, S, _ = h.shape
    q, k, v, fl = jnp.split(h @ w_qkvf, [Q_DIM, Q_DIM + KV_DIM, Q_DIM + 2 * KV_DIM], axis=-1)
    q = q.reshape(N, S, N_KV_HEADS, GROUP, HEAD_DIM)
    k = k.reshape(N, S, N_KV_HEADS, HEAD_DIM)
    v = v.reshape(N, S, N_KV_HEADS, HEAD_DIM)
    logf = jax.nn.log_sigmoid(fl.astype(jnp.float32) + b_f.astype(jnp.float32))
    return q, k, v, logf


def fox_prompt(h, w_qkvf, b_f, w_o):
    B, S, _ = h.shape
    q, k, v, logf = fox_project(h, w_qkvf, b_f)
    Ft = jnp.cumsum(logf, axis=1).reshape(B, S, N_KV_HEADS, GROUP).transpose(0, 2, 3, 1)
    kpos = jnp.arange(S)
    scale = HEAD_DIM ** -0.5

    def block(n):
        q0 = n * BLOCK
        qb = lax.dynamic_slice_in_dim(q, q0, BLOCK, axis=1)
        Fq = lax.dynamic_slice_in_dim(Ft, q0, BLOCK, axis=3)
        s = jnp.einsum('bqhgd,bkhd->bhgqk', qb, k).astype(jnp.float32) * scale
        s = s + Fq[..., :, None] - Ft[..., None, :]
        qpos = q0 + jnp.arange(BLOCK)
        s = jnp.where(kpos[None, :] <= qpos[:, None], s, -jnp.inf)
        p = jax.nn.softmax(s, axis=-1).astype(v.dtype)
        return jnp.einsum('bhgqk,bkhd->bqhgd', p, v)

    o = lax.map(block, jnp.arange(S // BLOCK))
    o = jnp.moveaxis(o, 0, 1).reshape(B, S, Q_DIM) @ w_o
    return o, k, v, logf


def fox_sample(h, j, cache_k, cache_v, cache_logf, page_table, w_qkvf, b_f, w_o):
    DB, T, _ = h.shape
    n_pages = page_table.shape[1]
    q, k, v, logf = fox_project(h, w_qkvf, b_f)
    scale = HEAD_DIM ** -0.5
    A = jnp.cumsum(logf, axis=1).reshape(DB, T, N_KV_HEADS, GROUP).transpose(0, 2, 3, 1)
    lf_past = cache_logf[j, page_table].astype(jnp.float32).reshape(DB, n_pages * PAGE_SIZE, N_HEADS)
    suffix = lax.cumsum(lf_past, axis=1, reverse=True) - lf_past
    suffix = suffix.reshape(DB, n_pages, PAGE_SIZE, N_KV_HEADS, GROUP).transpose(1, 0, 3, 4, 2)

    def page_partial(args):
        phys, bias = args
        kp = cache_k[j, phys]
        vp = cache_v[j, phys]
        s = jnp.einsum('bqhgd,bkhd->bhgqk', q, kp.astype(q.dtype)).astype(jnp.float32) * scale
        s = s + A[..., :, None] + bias[..., None, :]
        m = s.max(axis=-1)
        p = jnp.exp(s - m[..., None])
        return m, p.sum(axis=-1), jnp.einsum('bhgqk,bkhd->bhgqd', p, vp.astype(jnp.float32))

    m_p, l_p, acc_p = lax.map(page_partial, (page_table.T, suffix))
    s = jnp.einsum('bqhgd,bkhd->bhgqk', q, k).astype(jnp.float32) * scale
    s = s + A[..., :, None] - A[..., None, :]
    causal = jnp.arange(T)[:, None] >= jnp.arange(T)[None, :]
    s = jnp.where(causal, s, -jnp.inf)
    m_s = s.max(axis=-1)
    p_s = jnp.exp(s - m_s[..., None])
    l_s = p_s.sum(axis=-1)
    acc_s = jnp.einsum('bhgqk,bkhd->bhgqd', p_s, v.astype(jnp.float32))
    m_all = jnp.maximum(m_p.max(axis=0), m_s)
    w_p = jnp.exp(m_p - m_all[None])
    w_s = jnp.exp(m_s - m_all)
    l_tot = (l_p * w_p).sum(axis=0) + l_s * w_s
    acc = (acc_p * w_p[..., None]).sum(axis=0) + acc_s * w_s[..., None]
    o = (acc / l_tot[..., None]).transpose(0, 3, 1, 2, 4).reshape(DB, T, Q_DIM).astype(h.dtype)
    return o @ w_o, k, v, logf


def moe_ffn(h, j, w_router, b_router, w_g, w_u, w_d):
    lead = h.shape[:-1]
    x = h.reshape(-1, D_MODEL)
    n = x.shape[0]
    logits = (x @ w_router[j]).astype(jnp.float32) + b_router[j].astype(jnp.float32)
    top_v, top_i = lax.top_k(logits, TOP_K)
    gates = jax.nn.softmax(top_v, axis=-1)
    nk = n * TOP_K
    flat_e = top_i.reshape(nk)
    flat_t = jnp.repeat(jnp.arange(n, dtype=jnp.int32), TOP_K)
    flat_g = gates.reshape(nk)
    order = jnp.argsort(flat_e, stable=True)
    se, st, sg = flat_e[order], flat_t[order], flat_g[order]
    counts = jnp.bincount(flat_e, length=N_EXPERTS)
    starts = jnp.cumsum(counts) - counts
    padded = (counts + MOE_BLOCK - 1) // MOE_BLOCK * MOE_BLOCK
    pend = jnp.cumsum(padded)
    pstart = pend - padded
    dest = pstart[se] + jnp.arange(nk) - starts[se]
    n_blocks = -(-nk // MOE_BLOCK) + N_EXPERTS
    buf_t = jnp.full((n_blocks * MOE_BLOCK,), n, jnp.int32).at[dest].set(st)
    buf_g = jnp.zeros((n_blocks * MOE_BLOCK,), jnp.float32).at[dest].set(sg)
    block_e = jnp.minimum(jnp.searchsorted(pend, jnp.arange(n_blocks) * MOE_BLOCK, side='right'),
                          N_EXPERTS - 1)
    xpad = jnp.concatenate([x, jnp.zeros((1, D_MODEL), x.dtype)], axis=0)

    def run_block(args):
        e, toks = args
        return swiglu(xpad[toks], w_g[j, e], w_u[j, e], w_d[j, e])

    ys = lax.map(run_block, (block_e, buf_t.reshape(n_blocks, MOE_BLOCK)))
    ys = ys.reshape(-1, D_MODEL) * buf_g[:, None].astype(x.dtype)
    out = jnp.zeros((n + 1, D_MODEL), x.dtype).at[buf_t].add(ys)[:n]
    return out.reshape(*lead, D_MODEL)


def setup_inputs(seed: int = 0) -> dict:
    key = jax.random.key(seed)
    ks = jax.random.split(key, 32)
    f32 = jnp.float32
    n_pages = PAST_LEN // PAGE_SIZE
    n_used = DEC_BATCH * n_pages
    n_pool = (n_used * 5) // 4

    def nrm(k, shape, scale):
        return jax.random.normal(k, shape, f32) * scale

    x_prompt = nrm(ks[0], (BATCH, SEQ, D_MODEL), 1.0)
    x_sample = nrm(ks[1], (DEC_BATCH, DEC_SEQ, D_MODEL), 1.0)
    state_win_k = nrm(ks[2], (N_SWA_LAYERS, DEC_BATCH, WINDOW, N_KV_HEADS, HEAD_DIM), 1.0)
    state_win_v = nrm(ks[3], (N_SWA_LAYERS, DEC_BATCH, WINDOW, N_KV_HEADS, HEAD_DIM), 1.0)
    cache_k = nrm(ks[4], (N_FOX_LAYERS, n_pool, PAGE_SIZE, N_KV_HEADS, HEAD_DIM), 1.0)
    cache_v = nrm(ks[5], (N_FOX_LAYERS, n_pool, PAGE_SIZE, N_KV_HEADS, HEAD_DIM), 1.0)
    cache_logf = jax.nn.log_sigmoid(nrm(ks[6], (N_FOX_LAYERS, n_pool, PAGE_SIZE, N_HEADS), 1.0) + 4.0)
    page_table = jax.random.permutation(ks[7], n_pool)[:n_used].reshape(DEC_BATCH, n_pages).astype(jnp.int32)
    c_prompt = nrm(ks[8], (BATCH, D_MODEL), 1.0)
    c_sample = nrm(ks[9], (DEC_BATCH, D_MODEL), 1.0)
    ada_w = nrm(ks[10], (DEPTH, D_MODEL, 6 * D_MODEL), 0.5 * D_MODEL ** -0.5)
    ada_b = nrm(ks[11], (DEPTH, 6 * D_MODEL), 0.02)
    norm_g = 1.0 + nrm(ks[12], (DEPTH, 2, D_MODEL), 0.02)
    final_g = 1.0 + nrm(ks[13], (D_MODEL,), 0.02)
    swa_w_qkv = nrm(ks[14], (N_SWA_LAYERS, D_MODEL, Q_DIM + 2 * KV_DIM), D_MODEL ** -0.5)
    swa_sinks = nrm(ks[15], (N_SWA_LAYERS, N_HEADS), 1.0)
    swa_w_o = nrm(ks[16], (N_SWA_LAYERS, Q_DIM, D_MODEL), Q_DIM ** -0.5)
    fox_w_qkvf = nrm(ks[17], (N_FOX_LAYERS, D_MODEL, Q_DIM + 2 * KV_DIM + N_HEADS), D_MODEL ** -0.5)
    fox_b_f = jax.random.uniform(ks[18], (N_FOX_LAYERS, N_HEADS), f32, 2.0, 7.0)
    fox_w_o = nrm(ks[19], (N_FOX_LAYERS, Q_DIM, D_MODEL), Q_DIM ** -0.5)
    ffn_w_g = nrm(ks[20], (N_DENSE_LAYERS, D_MODEL, D_FF), D_MODEL ** -0.5)
    ffn_w_u = nrm(ks[21], (N_DENSE_LAYERS, D_MODEL, D_FF), D_MODEL ** -0.5)
    ffn_w_d = nrm(ks[22], (N_DENSE_LAYERS, D_FF, D_MODEL), D_FF ** -0.5)
    moe_w_router = nrm(ks[23], (N_MOE_LAYERS, D_MODEL, N_EXPERTS), D_MODEL ** -0.5)
    moe_b_router = nrm(ks[24], (N_MOE_LAYERS, N_EXPERTS), 0.01)
    moe_w_g = nrm(ks[25], (N_MOE_LAYERS, N_EXPERTS, D_MODEL, D_FF), D_MODEL ** -0.5)
    moe_w_u = nrm(ks[26], (N_MOE_LAYERS, N_EXPERTS, D_MODEL, D_FF), D_MODEL ** -0.5)
    moe_w_d = nrm(ks[27], (N_MOE_LAYERS, N_EXPERTS, D_FF, D_MODEL), D_FF ** -0.5)
    return {'x_prompt': x_prompt, 'x_sample': x_sample,
            'state_win_k': state_win_k, 'state_win_v': state_win_v,
            'cache_k': cache_k, 'cache_v': cache_v, 'cache_logf': cache_logf,
            'page_table': page_table, 'c_prompt': c_prompt, 'c_sample': c_sample,
            'ada_w': ada_w, 'ada_b': ada_b, 'norm_g': norm_g, 'final_g': final_g,
            'swa_w_qkv': swa_w_qkv, 'swa_sinks': swa_sinks, 'swa_w_o': swa_w_o,
            'fox_w_qkvf': fox_w_qkvf, 'fox_b_f': fox_b_f, 'fox_w_o': fox_w_o,
            'ffn_w_g': ffn_w_g, 'ffn_w_u': ffn_w_u, 'ffn_w_d': ffn_w_d,
            'moe_w_router': moe_w_router, 'moe_b_router': moe_b_router,
            'moe_w_g': moe_w_g, 'moe_w_u': moe_w_u, 'moe_w_d': moe_w_d}


def reference(x_prompt, x_sample, state_win_k, state_win_v, cache_k, cache_v, cache_logf,
              page_table, c_prompt, c_sample, ada_w, ada_b, norm_g, final_g,
              swa_w_qkv, swa_sinks, swa_w_o, fox_w_qkvf, fox_b_f, fox_w_o,
              ffn_w_g, ffn_w_u, ffn_w_d, moe_w_router, moe_b_router, moe_w_g, moe_w_u, moe_w_d):
    xp, xs = x_prompt, x_sample
    wk_p, wv_p, wk_s, wv_s = [], [], [], []
    fk_p, fv_p, fl_p, fk_s, fv_s, fl_s = [], [], [], [], [], []
    for i in range(DEPTH):
        j = i // N_MIXERS
        mp = ada_params(c_prompt, ada_w[i], ada_b[i])
        ms = ada_params(c_sample, ada_w[i], ada_b[i])
        hp = modulate(rmsnorm(xp, norm_g[i, 0]), mp[0], mp[1])
        hs = modulate(rmsnorm(xs, norm_g[i, 0]), ms[0], ms[1])
        if i % N_MIXERS == 0:
            op, kp_, vp_ = swa_prompt(hp, swa_w_qkv[j], swa_sinks[j], swa_w_o[j])
            os_, ks_, vs_ = swa_sample(hs, state_win_k[j], state_win_v[j],
                                       swa_w_qkv[j], swa_sinks[j], swa_w_o[j])
            wk_p.append(kp_)
            wv_p.append(vp_)
            wk_s.append(ks_)
            wv_s.append(vs_)
        else:
            op, kp_, vp_, lp_ = fox_prompt(hp, fox_w_qkvf[j], fox_b_f[j], fox_w_o[j])
            os_, ks_, vs_, ls_ = fox_sample(hs, j, cache_k, cache_v, cache_logf, page_table,
                                            fox_w_qkvf[j], fox_b_f[j], fox_w_o[j])
            fk_p.append(kp_)
            fv_p.append(vp_)
            fl_p.append(lp_)
            fk_s.append(ks_)
            fv_s.append(vs_)
            fl_s.append(ls_)
        xp = xp + mp[2][:, None, :] * op
        xs = xs + ms[2][:, None, :] * os_
        hp = modulate(rmsnorm(xp, norm_g[i, 1]), mp[3], mp[4])
        hs = modulate(rmsnorm(xs, norm_g[i, 1]), ms[3], ms[4])
        if i % 2 == 0:
            fp = swiglu(hp, ffn_w_g[j], ffn_w_u[j], ffn_w_d[j])
            fs = swiglu(hs, ffn_w_g[j], ffn_w_u[j], ffn_w_d[j])
        else:
            fp = moe_ffn(hp, j, moe_w_router, moe_b_router, moe_w_g, moe_w_u, moe_w_d)
            fs = moe_ffn(hs, j, moe_w_router, moe_b_router, moe_w_g, moe_w_u, moe_w_d)
        xp = xp + mp[5][:, None, :] * fp
        xs = xs + ms[5][:, None, :] * fs
    y_prompt = rmsnorm(xp, final_g)
    y_sample = rmsnorm(xs, final_g)
    return (y_prompt, y_sample,
            jnp.stack(wk_p), jnp.stack(wv_p), jnp.stack(wk_s), jnp.stack(wv_s),
            jnp.stack(fk_p), jnp.stack(fv_p), jnp.stack(fl_p),
            jnp.stack(fk_s), jnp.stack(fv_s), jnp.stack(fl_s))
```

```python
import functools

import jax
import jax.numpy as jnp
from jax import lax
from jax.experimental import pallas as pl
from jax.experimental.pallas import tpu as pltpu

F32 = jnp.float32
BF16 = jnp.bfloat16
I32 = jnp.int32

HEAD_DIM = 128
GROUP = 4
ROT_DIM = HEAD_DIM // 4
ROPE_THETA = 500000.0
WINDOW = 128
RMS_EPS = 1e-6
TOP_K = 2
LANES = 128
NEG = -1e30
HIGHEST = lax.Precision.HIGHEST

VMEM_LIMIT = 56 * 1024 * 1024

TILES = dict(
    ada_tn=512,
    norm_ts=256,
    mm_tm=1024, mm_tn=512,
    gu_tm=1024, gu_tn=256,
    dn_tn=512, dn_tk=3584,
    moe_tm=256, moe_gu_tn=512, moe_dn_tn=512, moe_dn_tk=7168,
    fox_tq=256,
    fox_pages=8,
    gather_rows=256,
    combine_rows=128,
)


def _tile(pref, dim):
    t = min(pref, dim)
    assert dim % t == 0, (pref, dim)
    return t


def _params(sem):
    return pltpu.CompilerParams(dimension_semantics=sem, vmem_limit_bytes=VMEM_LIMIT)


def _sigmoid(x):
    return 1.0 / (1.0 + jnp.exp(-x))


def _ada_kernel(c_ref, w_ref, b_ref, o_ref):
    c = c_ref[...]
    o_ref[...] = jnp.dot(c * _sigmoid(c), w_ref[...], preferred_element_type=F32, precision=HIGHEST) + b_ref[...]


def ada_params(c_all, ada_w, ada_b):
    L, D, N = ada_w.shape
    R = c_all.shape[0]
    tn = _tile(TILES["ada_tn"], N)
    return pl.pallas_call(
        _ada_kernel,
        grid=(L, N // tn),
        in_specs=[
            pl.BlockSpec((R, D), lambda l, n: (0, 0)),
            pl.BlockSpec((None, D, tn), lambda l, n: (l, 0, n)),
            pl.BlockSpec((None, 1, tn), lambda l, n: (l, 0, n)),
        ],
        out_specs=pl.BlockSpec((None, R, tn), lambda l, n: (l, 0, n)),
        out_shape=jax.ShapeDtypeStruct((L, R, N), F32),
        compiler_params=_params(("arbitrary", "arbitrary")),
        name="ada_params",
    )(c_all, ada_w, ada_b.reshape(L, 1, N))


def _norm_mod_kernel(*refs, resid, router, n_experts):
    it = iter(refs)
    x_ref = next(it)
    if resid:
        y_ref, gate_ref = next(it), next(it)
    g_ref, shift_ref, scale_ref = next(it), next(it), next(it)
    if router:
        wr_ref, br_ref = next(it), next(it)
    if resid:
        xo_ref = next(it)
    h_ref = next(it)
    if router:
        idx_ref, gts_ref = next(it), next(it)

    x = x_ref[...]
    if resid:
        x = x + gate_ref[...] * y_ref[...]
        xo_ref[...] = x
    ms = jnp.mean(x * x, axis=-1, keepdims=True)
    yn = x * lax.rsqrt(ms + RMS_EPS) * g_ref[...]
    h = yn * (1.0 + scale_ref[...]) + shift_ref[...]
    h_ref[...] = h.astype(h_ref.dtype)
    if router:
        logits = jnp.dot(h, wr_ref[...], preferred_element_type=F32, precision=HIGHEST) + br_ref[...]
        lane = lax.broadcasted_iota(I32, logits.shape, 1)
        logits = jnp.where(lane < n_experts, logits, NEG)
        v1 = jnp.max(logits, axis=-1, keepdims=True)
        i1 = jnp.min(jnp.where(logits == v1, lane, LANES), axis=-1, keepdims=True)
        rest = jnp.where(lane == i1, NEG, logits)
        v2 = jnp.max(rest, axis=-1, keepdims=True)
        i2 = jnp.min(jnp.where(rest == v2, lane, LANES), axis=-1, keepdims=True)
        e = jnp.exp(v2 - v1)
        g1 = 1.0 / (1.0 + e)
        g2 = e / (1.0 + e)
        idx_ref[...] = jnp.where(lane == 0, i1, jnp.where(lane == 1, i2, 0))
        gts_ref[...] = jnp.where(lane == 0, g1, jnp.where(lane == 1, g2, 0.0))


def norm_mod(x, g, shift, scale, *, y=None, gate=None, router=None, h_dtype=BF16):
    G, S, D = x.shape
    ts = _tile(TILES["norm_ts"], S)
    resid = y is not None

    def pspec(p):
        if p.shape[1] == 1:
            return pl.BlockSpec((None, 1, D), lambda b, i: (b, 0, 0))
        return pl.BlockSpec((None, ts, D), lambda b, i: (b, i, 0))

    xspec = pl.BlockSpec((None, ts, D), lambda b, i: (b, i, 0))
    args, in_specs = [x], [xspec]
    if resid:
        args += [y, gate]
        in_specs += [xspec, pspec(gate)]
    args += [g.reshape(1, D), shift, scale]
    in_specs += [pl.BlockSpec((1, D), lambda b, i: (0, 0)), pspec(shift), pspec(scale)]
    n_experts = 0
    if router is not None:
        w_r, b_r = router
        n_experts = w_r.shape[1]
        wr = jnp.zeros((D, LANES), F32).at[:, :n_experts].set(w_r)
        br = jnp.zeros((1, LANES), F32).at[0, :n_experts].set(b_r)
        args += [wr, br]
        in_specs += [pl.BlockSpec((D, LANES), lambda b, i: (0, 0)),
                     pl.BlockSpec((1, LANES), lambda b, i: (0, 0))]
    out_shape, out_specs = [], []
    if resid:
        out_shape.append(jax.ShapeDtypeStruct((G, S, D), F32))
        out_specs.append(xspec)
    out_shape.append(jax.ShapeDtypeStruct((G, S, D), h_dtype))
    out_specs.append(xspec)
    if router is not None:
        lspec = pl.BlockSpec((None, ts, LANES), lambda b, i: (b, i, 0))
        out_shape += [jax.ShapeDtypeStruct((G, S, LANES), I32), jax.ShapeDtypeStruct((G, S, LANES), F32)]
        out_specs += [lspec, lspec]
    outs = pl.pallas_call(
        functools.partial(_norm_mod_kernel, resid=resid, router=router is not None, n_experts=n_experts),
        grid=(G, S // ts),
        in_specs=in_specs,
        out_specs=out_specs,
        out_shape=out_shape,
        compiler_params=_params(("arbitrary", "arbitrary")),
        name="norm_mod",
    )(*args)
    outs = list(outs)
    x_new = outs.pop(0) if resid else None
    h = outs.pop(0)
    return x_new, h, (tuple(outs) if router is not None else None)


def _wdot(a_ref, w_ref, wb, refresh):
    if a_ref.dtype == F32:
        return jnp.dot(a_ref[...], w_ref[...], preferred_element_type=F32, precision=HIGHEST)

    @pl.when(refresh)
    def _():
        wb[0][...] = w_ref[...].astype(BF16)

    return jnp.dot(a_ref[...], wb[0][...], preferred_element_type=F32)


def _wscratch(a, shape, n=1):
    return [] if a.dtype == F32 else [pltpu.VMEM(shape, BF16)] * n


def _mm_kernel(a_ref, w_ref, o_ref, *wb):
    o_ref[...] = _wdot(a_ref, w_ref, wb, pl.program_id(1) == 0).astype(o_ref.dtype)


def matmul(a, w, n_cols=None, out_dtype=F32):
    M, K = a.shape
    N = w.shape[1] if n_cols is None else n_cols
    tm = _tile(TILES["mm_tm"], M)
    tn = _tile(TILES["mm_tn"], N)
    return pl.pallas_call(
        _mm_kernel,
        grid=(N // tn, M // tm),
        in_specs=[
            pl.BlockSpec((tm, K), lambda n, m: (m, 0)),
            pl.BlockSpec((K, tn), lambda n, m: (0, n)),
        ],
        out_specs=pl.BlockSpec((tm, tn), lambda n, m: (m, n)),
        out_shape=jax.ShapeDtypeStruct((M, N), out_dtype),
        scratch_shapes=_wscratch(a, (K, tn)),
        compiler_params=_params(("arbitrary", "arbitrary")),
        name="matmul",
    )(a, w)


def _logf_kernel(a_ref, w_ref, b_ref, o_ref):
    if a_ref.dtype == F32:
        z = jnp.dot(a_ref[...], w_ref[...], preferred_element_type=F32, precision=HIGHEST)
    else:
        z = jnp.dot(a_ref[...], w_ref[...].astype(BF16), preferred_element_type=F32)
    z = z + b_ref[...]
    o_ref[...] = jnp.minimum(z, 0.0) - jnp.log(1.0 + jnp.exp(-jnp.abs(z)))


def logf_proj(a, w_f, b_f):
    M, K = a.shape
    nh = w_f.shape[1]
    wp = jnp.zeros((K, LANES), F32).at[:, :nh].set(w_f)
    bp = jnp.zeros((1, LANES), F32).at[0, :nh].set(b_f)
    tm = _tile(TILES["mm_tm"], M)
    return pl.pallas_call(
        _logf_kernel,
        grid=(M // tm,),
        in_specs=[
            pl.BlockSpec((tm, K), lambda m: (m, 0)),
            pl.BlockSpec((K, LANES), lambda m: (0, 0)),
            pl.BlockSpec((1, LANES), lambda m: (0, 0)),
        ],
        out_specs=pl.BlockSpec((tm, LANES), lambda m: (m, 0)),
        out_shape=jax.ShapeDtypeStruct((M, LANES), F32),
        compiler_params=_params(("arbitrary",)),
        name="logf_proj",
    )(a, wp, bp)


def _new_group(te_ref, m):
    prev = te_ref[jnp.maximum(m - 1, 0)]
    return jnp.logical_or(m == 0, te_ref[m] != prev)


def _gateup_kernel(te_ref, a_ref, wg_ref, wu_ref, o_ref, *wb):
    refresh = _new_group(te_ref, pl.program_id(1))
    g = _wdot(a_ref, wg_ref, wb[:1], refresh)
    u = _wdot(a_ref, wu_ref, wb[1:], refresh)
    o_ref[...] = (g * _sigmoid(g) * u).astype(o_ref.dtype)


def gateup(a, w_g, w_u, tile_expert, tm, tn):
    M, K = a.shape
    F = w_g.shape[2]
    tn = _tile(tn, F)
    wspec = pl.BlockSpec((None, K, tn), lambda n, m, te: (te[m], 0, n))
    return pl.pallas_call(
        _gateup_kernel,
        grid_spec=pltpu.PrefetchScalarGridSpec(
            num_scalar_prefetch=1,
            grid=(F // tn, M // tm),
            in_specs=[pl.BlockSpec((tm, K), lambda n, m, te: (m, 0)), wspec, wspec],
            out_specs=pl.BlockSpec((tm, tn), lambda n, m, te: (m, n)),
            scratch_shapes=_wscratch(a, (K, tn), 2),
        ),
        out_shape=jax.ShapeDtypeStruct((M, F), a.dtype),
        compiler_params=_params(("arbitrary", "arbitrary")),
        name="gateup",
    )(tile_expert, a, w_g, w_u)


def _down_kernel(te_ref, a_ref, w_ref, *refs, has_partial):
    if has_partial:
        p_ref, o_ref, *wb = refs
    else:
        o_ref, *wb = refs
    acc = _wdot(a_ref, w_ref, wb, _new_group(te_ref, pl.program_id(1)))
    if has_partial:
        acc = p_ref[...] + acc
    o_ref[...] = acc


def down(a, w_d, tile_expert, tm, tn, tk):
    M, F = a.shape
    D = w_d.shape[2]
    tn = _tile(tn, D)
    tk = _tile(tk, F)
    ospec = pl.BlockSpec((tm, tn), lambda n, m, te: (m, n))
    y = None
    for k in range(F // tk):
        in_specs = [
            pl.BlockSpec((tm, tk), lambda n, m, te, k=k: (m, k)),
            pl.BlockSpec((None, tk, tn), lambda n, m, te, k=k: (te[m], k, n)),
        ]
        args = [tile_expert, a, w_d]
        if y is not None:
            in_specs.append(ospec)
            args.append(y)
        y = pl.pallas_call(
            functools.partial(_down_kernel, has_partial=y is not None),
            grid_spec=pltpu.PrefetchScalarGridSpec(
                num_scalar_prefetch=1,
                grid=(D // tn, M // tm),
                in_specs=in_specs,
                out_specs=ospec,
                scratch_shapes=_wscratch(a, (tk, tn)),
            ),
            out_shape=jax.ShapeDtypeStruct((M, D), F32),
            compiler_params=_params(("arbitrary", "arbitrary")),
            name="down",
        )(*args)
    return y


def rope_tables(pos):
    inv_freq = ROPE_THETA ** (-jnp.arange(0, ROT_DIM, 2, dtype=F32) / ROT_DIM)
    ang = pos.astype(F32)[:, None] * inv_freq[None, :]
    cos, sin = jnp.cos(ang), jnp.sin(ang)
    n = pos.shape[0]
    half = ROT_DIM // 2
    ct = jnp.concatenate([cos, cos, jnp.ones((n, HEAD_DIM - ROT_DIM), F32)], axis=1)
    s1 = jnp.concatenate([-sin, jnp.zeros((n, HEAD_DIM - half), F32)], axis=1)
    s2 = jnp.concatenate([jnp.zeros((n, half), F32), sin, jnp.zeros((n, HEAD_DIM - ROT_DIM), F32)], axis=1)
    return ct, s1, s2


def _rope_kernel(q_ref, k_ref, c_ref, s1_ref, s2_ref, qo_ref, ko_ref):
    c, s1, s2 = c_ref[...], s1_ref[...], s2_ref[...]
    half = ROT_DIM // 2

    def rot(x):
        return x * c + pltpu.roll(x, HEAD_DIM - half, 1) * s1 + pltpu.roll(x, half, 1) * s2

    for hd in range(q_ref.shape[1] // HEAD_DIM):
        sl = slice(hd * HEAD_DIM, (hd + 1) * HEAD_DIM)
        qo_ref[:, sl] = rot(q_ref[:, sl]).astype(qo_ref.dtype)
    for hd in range(k_ref.shape[1] // HEAD_DIM):
        sl = slice(hd * HEAD_DIM, (hd + 1) * HEAD_DIM)
        ko_ref[:, sl] = rot(k_ref[:, sl])


def rope(qkv, tables, q_dim, kv_dim, q_dtype):
    M = qkv.shape[0]
    S = tables[0].shape[0]
    ts = _tile(TILES["norm_ts"], S)
    ns = S // ts
    tspec = pl.BlockSpec((ts, HEAD_DIM), lambda i: (i % ns, 0))
    return pl.pallas_call(
        _rope_kernel,
        grid=(M // ts,),
        in_specs=[
            pl.BlockSpec((ts, q_dim), lambda i: (i, 0)),
            pl.BlockSpec((ts, kv_dim), lambda i: (i, q_dim // kv_dim)),
            tspec, tspec, tspec,
        ],
        out_specs=[pl.BlockSpec((ts, q_dim), lambda i: (i, 0)), pl.BlockSpec((ts, kv_dim), lambda i: (i, 0))],
        out_shape=[jax.ShapeDtypeStruct((M, q_dim), q_dtype), jax.ShapeDtypeStruct((M, kv_dim), F32)],
        compiler_params=_params(("arbitrary",)),
        name="rope",
    )(qkv, qkv, *tables)


def _sink_softmax(s, sink, dtype=BF16):
    m = jnp.maximum(jnp.max(s, axis=-1, keepdims=True), sink)
    p = jnp.exp(s - m)
    den = jnp.sum(p, axis=-1, keepdims=True) + jnp.exp(sink - m)
    return (p / den).astype(dtype)


def _swa_prompt_kernel(sink_ref, q_ref, kp_ref, kc_ref, vp_ref, vc_ref, o_ref):
    i = pl.program_id(1)
    h = pl.program_id(2)
    blk = WINDOW
    kb = jnp.concatenate([kp_ref[...], kc_ref[...]], axis=0).astype(BF16)
    vb = jnp.concatenate([vp_ref[...], vc_ref[...]], axis=0).astype(BF16)
    qs = jnp.concatenate([q_ref[:, g * HEAD_DIM:(g + 1) * HEAD_DIM] for g in range(GROUP)], axis=0)
    s = lax.dot_general(qs, kb, (((1,), (1,)), ((), ())), preferred_element_type=F32) * (HEAD_DIM ** -0.5)
    qi = lax.broadcasted_iota(I32, (blk, 2 * blk), 0)
    kj = lax.broadcasted_iota(I32, (blk, 2 * blk), 1)
    lo = jnp.where(i > 0, 0, blk)
    valid = jnp.logical_and(kj >= jnp.maximum(qi, lo), kj <= qi + blk)
    for g in range(GROUP):
        sg = jnp.where(valid, s[g * blk:(g + 1) * blk], NEG)
        p = _sink_softmax(sg, sink_ref[h * GROUP + g])
        o = jnp.dot(p, vb, preferred_element_type=F32)
        o_ref[:, g * HEAD_DIM:(g + 1) * HEAD_DIM] = o.astype(o_ref.dtype)


def swa_prompt(q_rot, k_rot, qkv, sinks, B, S):
    M, Q = q_rot.shape
    KV = k_rot.shape[1]
    kvh = KV // HEAD_DIM
    nb = S // WINDOW
    gq = GROUP * HEAD_DIM
    v_col0 = (Q + KV) // HEAD_DIM
    cur = lambda b, i, h: (b * nb + i, h)
    prev = lambda b, i, h: (b * nb + jnp.maximum(i - 1, 0), h)
    vcur = lambda b, i, h: (b * nb + i, v_col0 + h)
    vprev = lambda b, i, h: (b * nb + jnp.maximum(i - 1, 0), v_col0 + h)
    blk = (WINDOW, HEAD_DIM)
    return pl.pallas_call(
        _swa_prompt_kernel,
        grid=(B, nb, kvh),
        in_specs=[
            pl.BlockSpec(memory_space=pltpu.SMEM),
            pl.BlockSpec((WINDOW, gq), cur),
            pl.BlockSpec(blk, prev), pl.BlockSpec(blk, cur),
            pl.BlockSpec(blk, vprev), pl.BlockSpec(blk, vcur),
        ],
        out_specs=pl.BlockSpec((WINDOW, gq), cur),
        out_shape=jax.ShapeDtypeStruct((M, Q), BF16),
        compiler_params=_params(("arbitrary", "arbitrary", "arbitrary")),
        name="swa_prompt",
    )(sinks, q_rot, k_rot, k_rot, qkv, qkv)


def _swa_sample_kernel(sink_ref, q_ref, wk_ref, wv_ref, kn_ref, vn_ref, o_ref, *, n_new):
    kvh, rows, _ = q_ref.shape
    npad = kn_ref.shape[0]
    nk = WINDOW + npad
    r = lax.broadcasted_iota(I32, (rows, nk), 0)
    c = lax.broadcasted_iota(I32, (rows, nk), 1)
    t = r % n_new
    valid = jnp.logical_and(c >= t, c <= t + WINDOW)
    rg = lax.broadcasted_iota(I32, (rows, 1), 0) // n_new
    for h in range(kvh):
        sl = slice(h * HEAD_DIM, (h + 1) * HEAD_DIM)
        kb = jnp.concatenate([wk_ref[:, sl], kn_ref[:, sl]], axis=0)
        vb = jnp.concatenate([wv_ref[:, sl], vn_ref[:, sl]], axis=0)
        s = lax.dot_general(q_ref[h], kb, (((1,), (1,)), ((), ())),
                            preferred_element_type=F32, precision=HIGHEST) * (HEAD_DIM ** -0.5)
        s = jnp.where(valid, s, NEG)
        sink = jnp.zeros((rows, 1), F32)
        for g in range(GROUP):
            sink = jnp.where(rg == g, sink_ref[h * GROUP + g], sink)
        p = _sink_softmax(s, sink, F32)
        o_ref[h] = jnp.dot(p, vb, preferred_element_type=F32, precision=HIGHEST)


def swa_sample(q_hr, win_k, win_v, k_new, v_new, sinks, n_new):
    DB, kvh, rows, _ = q_hr.shape
    KV = win_k.shape[2]
    npad = k_new.shape[1]
    qspec = pl.BlockSpec((None, kvh, rows, HEAD_DIM), lambda b: (b, 0, 0, 0))
    wspec = pl.BlockSpec((None, WINDOW, KV), lambda b: (b, 0, 0))
    nspec = pl.BlockSpec((None, npad, KV), lambda b: (b, 0, 0))
    return pl.pallas_call(
        functools.partial(_swa_sample_kernel, n_new=n_new),
        grid=(DB,),
        in_specs=[pl.BlockSpec(memory_space=pltpu.SMEM), qspec, wspec, wspec, nspec, nspec],
        out_specs=qspec,
        out_shape=jax.ShapeDtypeStruct((DB, kvh, rows, HEAD_DIM), F32),
        compiler_params=_params(("arbitrary",)),
        name="swa_sample",
    )(sinks, q_hr, win_k, win_v, k_new, v_new)


def _cumsum_kernel(x_ref, o_ref):
    n = x_ref.shape[0] // LANES
    r = lax.broadcasted_iota(I32, (LANES, LANES), 0)
    c = lax.broadcasted_iota(I32, (LANES, LANES), 1)
    tri = jnp.where(c <= r, 1.0, 0.0).astype(F32)
    carry = jnp.zeros((1, LANES), F32)
    for b in range(n):
        xb = x_ref[b * LANES:(b + 1) * LANES, :]
        ob = jnp.dot(tri, xb, preferred_element_type=F32, precision=HIGHEST) + carry
        o_ref[b * LANES:(b + 1) * LANES, :] = ob
        carry = ob[LANES - 1:LANES, :]


def seq_cumsum(x, B, S):
    return pl.pallas_call(
        _cumsum_kernel,
        grid=(B,),
        in_specs=[pl.BlockSpec((S, LANES), lambda b: (b, 0))],
        out_specs=pl.BlockSpec((S, LANES), lambda b: (b, 0)),
        out_shape=jax.ShapeDtypeStruct(x.shape, F32),
        compiler_params=_params(("arbitrary",)),
        name="seq_cumsum",
    )(x)


def _fox_prompt_kernel(q_ref, k_ref, v_ref, fq_ref, fk_ref, o_ref, *, tq):
    qi = pl.program_id(2)
    scale = HEAD_DIM ** -0.5
    qs = jnp.concatenate([q_ref[:, g * HEAD_DIM:(g + 1) * HEAD_DIM] for g in range(GROUP)],
                         axis=0).astype(BF16)
    fq = jnp.concatenate([fq_ref[:, g:g + 1] for g in range(GROUP)], axis=0)
    rows = GROUP * tq

    def scores(j):
        start = pl.multiple_of(j * tq, tq)
        kb = k_ref[pl.ds(start, tq), :].astype(BF16)
        vb = v_ref[pl.ds(start, tq), :].astype(BF16)
        fk = fk_ref[:, pl.ds(start, tq)]
        fkb = jnp.concatenate([jnp.broadcast_to(fk[g:g + 1, :], (tq, tq)) for g in range(GROUP)], axis=0)
        s = lax.dot_general(qs, kb, (((1,), (1,)), ((), ())), preferred_element_type=F32) * scale
        return s + fq - fkb, vb

    def update(carry, s, vb):
        m, l, acc = carry
        m_new = jnp.maximum(m, jnp.max(s, axis=-1, keepdims=True))
        alpha = jnp.exp(m - m_new)
        p = jnp.exp(s - m_new)
        l = alpha * l + jnp.sum(p, axis=-1, keepdims=True)
        acc = alpha * acc + jnp.dot(p.astype(BF16), vb, preferred_element_type=F32)
        return m_new, l, acc

    def body(j, carry):
        s, vb = scores(j)
        return update(carry, s, vb)

    init = (jnp.full((rows, 1), NEG, F32), jnp.zeros((rows, 1), F32), jnp.zeros((rows, HEAD_DIM), F32))
    carry = lax.fori_loop(0, qi, body, init)
    s, vb = scores(qi)
    qp = lax.broadcasted_iota(I32, (rows, tq), 0) % tq
    kp = lax.broadcasted_iota(I32, (rows, tq), 1)
    s = jnp.where(kp <= qp, s, NEG)
    m, l, acc = update(carry, s, vb)
    o = acc / l
    for g in range(GROUP):
        o_ref[:, g * HEAD_DIM:(g + 1) * HEAD_DIM] = o[g * tq:(g + 1) * tq].astype(o_ref.dtype)


def fox_prompt(qkv, f_col, f_row, B, S, q_dim, kv_dim):
    M = qkv.shape[0]
    kvh = kv_dim // HEAD_DIM
    tq = _tile(TILES["fox_tq"], S)
    nq = S // tq
    gq = GROUP * HEAD_DIM
    k0 = q_dim // HEAD_DIM
    v0 = (q_dim + kv_dim) // HEAD_DIM
    return pl.pallas_call(
        functools.partial(_fox_prompt_kernel, tq=tq),
        grid=(B, kvh, nq),
        in_specs=[
            pl.BlockSpec((tq, gq), lambda b, h, i: (b * nq + i, h)),
            pl.BlockSpec((S, HEAD_DIM), lambda b, h, i: (b, k0 + h)),
            pl.BlockSpec((S, HEAD_DIM), lambda b, h, i: (b, v0 + h)),
            pl.BlockSpec((None, None, tq, GROUP), lambda b, h, i: (b, h, i, 0)),
            pl.BlockSpec((None, None, GROUP, S), lambda b, h, i: (b, h, 0, 0)),
        ],
        out_specs=pl.BlockSpec((tq, gq), lambda b, h, i: (b * nq + i, h)),
        out_shape=jax.ShapeDtypeStruct((M, q_dim), BF16),
        compiler_params=_params(("arbitrary", "arbitrary", "arbitrary")),
        name="fox_prompt",
    )(qkv, qkv, qkv, f_col, f_row)


def _fox_sample_kernel(pt_ref, qbd_ref, *refs, n_pages_step, n_new, n_heads):
    C = n_pages_step
    k_refs, v_refs, lf_refs = refs[:C], refs[C:2 * C], refs[2 * C:3 * C]
    kn_ref, vn_ref, lfn_ref, o_ref, kb_ref, vb_ref, sfx_ref, m_ref, l_ref, acc_ref, carry_ref = refs[3 * C:]
    c_idx = pl.program_id(1)
    n_steps = pl.num_programs(1)
    page = k_refs[0].shape[0]
    ncol = qbd_ref.shape[1]
    npad = kn_ref.shape[0]
    scale = HEAD_DIM ** -0.5
    kvh = qbd_ref.shape[0] // HEAD_DIM
    rows_h = ncol // kvh

    col = lax.broadcasted_iota(I32, (n_heads, ncol), 1)
    hg = lax.broadcasted_iota(I32, (n_heads, ncol), 0)
    expand = jnp.where(col // n_new == hg, 1.0, 0.0).astype(F32)

    @pl.when(c_idx == 0)
    def _():
        m_ref[...] = jnp.full(m_ref.shape, NEG, F32)
        l_ref[...] = jnp.zeros(l_ref.shape, F32)
        acc_ref[...] = jnp.zeros(acc_ref.shape, F32)
        carry_ref[...] = jnp.zeros(carry_ref.shape, F32)

    lfn = jnp.dot(lfn_ref[...], expand, preferred_element_type=F32, precision=HIGHEST)
    rr = lax.broadcasted_iota(I32, (npad, npad), 0)
    cc = lax.broadcasted_iota(I32, (npad, npad), 1)
    a_key = jnp.dot(jnp.where(cc <= rr, 1.0, 0.0).astype(F32), lfn,
                    preferred_element_type=F32, precision=HIGHEST)
    srow = lax.broadcasted_iota(I32, (npad, ncol), 0)
    tcol = lax.broadcasted_iota(I32, (npad, ncol), 1) % n_new
    a_row = jnp.sum(jnp.where(srow == tcol, a_key, 0.0), axis=0, keepdims=True)

    def col_of(row_vec):
        return jnp.transpose(jnp.broadcast_to(row_vec, (ncol, ncol)))[:, 0:1]

    def update(s, vb):
        m = m_ref[0:1, :]
        m_new = jnp.maximum(m, jnp.max(s, axis=0, keepdims=True))
        alpha = jnp.exp(m - m_new)
        p = jnp.exp(s - m_new)
        l_ref[0:1, :] = alpha * l_ref[0:1, :] + jnp.sum(p, axis=0, keepdims=True)
        pv = lax.dot_general(p.astype(BF16), vb, (((0,), (0,)), ((), ())), preferred_element_type=F32)
        acc_ref[...] = col_of(alpha) * acc_ref[...] + pv
        m_ref[0:1, :] = m_new

    ur = lax.broadcasted_iota(I32, (page, page), 0)
    uc = lax.broadcasted_iota(I32, (page, page), 1)
    upper = jnp.where(uc > ur, 1.0, 0.0).astype(F32)
    carry = carry_ref[0:1, :]
    for i in range(C):
        lfe = jnp.dot(lf_refs[i][...], expand, preferred_element_type=F32, precision=HIGHEST)
        sfx_ref[i * page:(i + 1) * page, :] = (
            jnp.dot(upper, lfe, preferred_element_type=F32, precision=HIGHEST) + carry)
        carry = carry + jnp.sum(lfe, axis=0, keepdims=True)
        kb_ref[i * page:(i + 1) * page, :] = k_refs[i][...].astype(BF16)
        vb_ref[i * page:(i + 1) * page, :] = v_refs[i][...].astype(BF16)
    carry_ref[0:1, :] = carry

    s = jnp.dot(kb_ref[...], qbd_ref[...], preferred_element_type=F32) * scale + a_row + sfx_ref[...]
    update(s, vb_ref[...])

    @pl.when(c_idx == n_steps - 1)
    def _():
        s2 = jnp.dot(kn_ref[...].astype(BF16), qbd_ref[...], preferred_element_type=F32) * scale + a_row - a_key
        s2 = jnp.where(srow <= tcol, s2, NEG)
        update(s2, vn_ref[...].astype(BF16))
        o_full = acc_ref[...] / col_of(l_ref[0:1, :])
        for h in range(kvh):
            o_ref[h * rows_h:(h + 1) * rows_h, :] = o_full[h * rows_h:(h + 1) * rows_h,
                                                           h * HEAD_DIM:(h + 1) * HEAD_DIM]


def fox_sample(qbd, cache_k, cache_v, cache_logf, page_table, k_new, v_new, lf_new, n_new):
    DB, KV, ncol = qbd.shape
    page = cache_k.shape[1]
    n_heads = cache_logf.shape[2]
    n_pages = page_table.shape[1]
    npad = k_new.shape[1]
    C = _tile(TILES["fox_pages"], n_pages)
    n_steps = n_pages // C

    def page_map(i):
        return lambda b, c, pt: (pt[b * n_pages + (n_pages - 1 - (c * C + i))], 0, 0)

    kspecs = [pl.BlockSpec((None, page, KV), page_map(i)) for i in range(C)]
    lspecs = [pl.BlockSpec((None, page, n_heads), page_map(i)) for i in range(C)]
    per_seq = lambda b, c, pt: (b, 0, 0)
    return pl.pallas_call(
        functools.partial(_fox_sample_kernel, n_pages_step=C, n_new=n_new, n_heads=n_heads),
        grid_spec=pltpu.PrefetchScalarGridSpec(
            num_scalar_prefetch=1,
            grid=(DB, n_steps),
            in_specs=[pl.BlockSpec((None, KV, ncol), per_seq)] + kspecs + kspecs + lspecs + [
                pl.BlockSpec((None, npad, KV), per_seq),
                pl.BlockSpec((None, npad, KV), per_seq),
                pl.BlockSpec((None, npad, n_heads), per_seq),
            ],
            out_specs=pl.BlockSpec((None, ncol, HEAD_DIM), per_seq),
            scratch_shapes=[
                pltpu.VMEM((C * page, KV), BF16), pltpu.VMEM((C * page, KV), BF16),
                pltpu.VMEM((C * page, ncol), F32),
                pltpu.VMEM((8, ncol), F32), pltpu.VMEM((8, ncol), F32),
                pltpu.VMEM((ncol, KV), F32), pltpu.VMEM((8, ncol), F32),
            ],
        ),
        out_shape=jax.ShapeDtypeStruct((DB, ncol, HEAD_DIM), F32),
        compiler_params=_params(("arbitrary", "arbitrary")),
        name="fox_sample",
    )(page_table.reshape(-1), qbd, *([cache_k] * C), *([cache_v] * C), *([cache_logf] * C),
      k_new, v_new, lf_new)


def _gather_kernel(idx_ref, x_hbm, o_ref, buf_ref, sem):
    rows = buf_ref.shape[0]
    base = pl.program_id(0) * rows

    def row_copy(r, src_row):
        return pltpu.make_async_copy(x_hbm.at[pl.ds(src_row, 1), :], buf_ref.at[pl.ds(r, 1), :], sem)

    def start(r, c):
        row_copy(r, idx_ref[base + r]).start()
        return c

    def wait(r, c):
        row_copy(r, 0).wait()
        return c

    lax.fori_loop(0, rows, start, 0)
    lax.fori_loop(0, rows, wait, 0)
    o_ref[...] = buf_ref[...].astype(o_ref.dtype)


def gather_rows(x, src_rows):
    D = x.shape[1]
    P = src_rows.shape[0]
    rows = _tile(TILES["gather_rows"], P)
    return pl.pallas_call(
        _gather_kernel,
        grid_spec=pltpu.PrefetchScalarGridSpec(
            num_scalar_prefetch=1,
            grid=(P // rows,),
            in_specs=[pl.BlockSpec(memory_space=pl.ANY)],
            out_specs=pl.BlockSpec((rows, D), lambda i, idx: (i, 0)),
            scratch_shapes=[pltpu.VMEM((rows, D), F32), pltpu.SemaphoreType.DMA(())],
        ),
        out_shape=jax.ShapeDtypeStruct((P, D), BF16),
        compiler_params=_params(("arbitrary",)),
        name="gather_rows",
    )(src_rows, x)


def _combine_kernel(d0_ref, d1_ref, x_ref, y_hbm, gts_ref, gate_ref, fg_ref, o_ref, b0_ref, b1_ref, sem0, sem1,
                    *, seq_rows):
    rows = b0_ref.shape[0]
    base = pl.program_id(0) * seq_rows + pl.program_id(1) * rows

    def row_copy(d_ref_row, buf, r, sem):
        return pltpu.make_async_copy(y_hbm.at[pl.ds(d_ref_row, 1), :], buf.at[pl.ds(r, 1), :], sem)

    def start(r, c):
        row_copy(d0_ref[base + r], b0_ref, r, sem0).start()
        row_copy(d1_ref[base + r], b1_ref, r, sem1).start()
        return c

    def wait(r, c):
        row_copy(0, b0_ref, r, sem0).wait()
        row_copy(0, b1_ref, r, sem1).wait()
        return c

    lax.fori_loop(0, rows, start, 0)
    lax.fori_loop(0, rows, wait, 0)
    g = gts_ref[...]
    f = g[:, 0:1] * b0_ref[...] + g[:, 1:2] * b1_ref[...]
    x = x_ref[...] + gate_ref[...] * f
    ms = jnp.mean(x * x, axis=-1, keepdims=True)
    o_ref[...] = x * lax.rsqrt(ms + RMS_EPS) * fg_ref[...]


def combine(x, y_sorted, dest, gates, gate, final_g):
    G, S, D = x.shape
    rows = _tile(TILES["combine_rows"], S)
    xspec = pl.BlockSpec((None, rows, D), lambda b, i, d0, d1: (b, i, 0))
    if gate.shape[1] == 1:
        gspec = pl.BlockSpec((None, 1, D), lambda b, i, d0, d1: (b, 0, 0))
    else:
        gspec = xspec
    return pl.pallas_call(
        functools.partial(_combine_kernel, seq_rows=S),
        grid_spec=pltpu.PrefetchScalarGridSpec(
            num_scalar_prefetch=2,
            grid=(G, S // rows),
            in_specs=[
                xspec,
                pl.BlockSpec(memory_space=pl.ANY),
                pl.BlockSpec((None, rows, LANES), lambda b, i, d0, d1: (b, i, 0)),
                gspec,
                pl.BlockSpec((1, D), lambda b, i, d0, d1: (0, 0)),
            ],
            out_specs=xspec,
            scratch_shapes=[pltpu.VMEM((rows, D), F32), pltpu.VMEM((rows, D), F32),
                            pltpu.SemaphoreType.DMA(()), pltpu.SemaphoreType.DMA(())],
        ),
        out_shape=jax.ShapeDtypeStruct((G, S, D), F32),
        compiler_params=_params(("arbitrary", "arbitrary")),
        name="combine",
    )(dest[:, 0], dest[:, 1], x, y_sorted, gates, gate, final_g.reshape(1, D))


def dispatch_plan(top_idx, n_experts, tm):
    T = top_idx.shape[0]
    nk = T * TOP_K
    n_tiles = -(-nk // tm) + n_experts
    P = n_tiles * tm
    flat_e = top_idx.reshape(nk)
    onehot = (flat_e[:, None] == jnp.arange(n_experts, dtype=I32)[None, :]).astype(I32)
    ranks = jnp.cumsum(onehot, axis=0) - onehot
    rank = jnp.sum(ranks * onehot, axis=1)
    counts = jnp.sum(onehot, axis=0)
    padded = (counts + tm - 1) // tm * tm
    pend = jnp.cumsum(padded)
    pstart = pend - padded
    dest = pstart[flat_e] + rank
    src_tok = jnp.zeros((P,), I32).at[dest].set(jnp.arange(nk, dtype=I32) // TOP_K)
    tile_start = jnp.arange(n_tiles, dtype=I32) * tm
    tile_expert = jnp.minimum(jnp.searchsorted(pend, tile_start, side="right"), n_experts - 1).astype(I32)
    return src_tok, dest.reshape(T, TOP_K), tile_expert


def kernel(x_prompt, x_sample, state_win_k, state_win_v, cache_k, cache_v, cache_logf, page_table, c_prompt, c_sample, ada_w, ada_b, norm_g, final_g, swa_w_qkv, swa_sinks, swa_w_o, fox_w_qkvf, fox_b_f, fox_w_o, ffn_w_g, ffn_w_u, ffn_w_d, moe_w_router, moe_b_router, moe_w_g, moe_w_u, moe_w_d):
    B, S, D = x_prompt.shape
    DB, T, _ = x_sample.shape
    n_heads = D // HEAD_DIM
    kvh = n_heads // GROUP
    q_dim, kv_dim = n_heads * HEAD_DIM, kvh * HEAD_DIM
    n_experts = moe_w_router.shape[2]
    n_pages = page_table.shape[1]
    page = cache_k.shape[2]
    past = n_pages * page
    MP, MS = B * S, DB * T
    npad = 16
    assert T <= npad and n_heads * T == LANES

    rpad = -(-(B + DB) // 16) * 16
    c_all = jnp.zeros((rpad, D), F32).at[:B].set(c_prompt).at[B:B + DB].set(c_sample)
    mods = ada_params(c_all, ada_w, ada_b)

    def mod_p(i, k):
        return mods[i, :B, k * D:(k + 1) * D].reshape(B, 1, D)

    def mod_s(i, k):
        return jnp.repeat(mods[i, B:B + DB, k * D:(k + 1) * D], T, axis=0).reshape(1, MS, D)

    xp = x_prompt
    xs = x_sample.reshape(1, MS, D)
    zeros_te_p = jnp.zeros((MP // _tile(TILES["gu_tm"], MP),), I32)
    zeros_te_s = jnp.zeros((1,), I32)

    _, hp, _ = norm_mod(xp, norm_g[0, 0], mod_p(0, 0), mod_p(0, 1))
    _, hs, _ = norm_mod(xs, norm_g[0, 0], mod_s(0, 0), mod_s(0, 1), h_dtype=F32)
    qkv_p = matmul(hp.reshape(MP, D), swa_w_qkv[0])
    qkv_s = matmul(hs.reshape(MS, D), swa_w_qkv[0])
    qp_rot, kp_rot = rope(qkv_p, rope_tables(jnp.arange(S)), q_dim, kv_dim, BF16)
    pos_s = past + (jnp.arange(MS) % T)
    qs_rot, ks_rot = rope(qkv_s, rope_tables(pos_s), q_dim, kv_dim, F32)
    o_p = swa_prompt(qp_rot, kp_rot, qkv_p, swa_sinks[0], B, S)

    def to_hr(q):
        return q.reshape(DB, T, kvh, GROUP, HEAD_DIM).transpose(0, 2, 3, 1, 4).reshape(DB, kvh, GROUP * T, HEAD_DIM)

    def from_hr(o):
        return o.reshape(DB, kvh, GROUP, T, HEAD_DIM).transpose(0, 3, 1, 2, 4).reshape(MS, q_dim)

    def pad_new(a):
        return jnp.zeros((DB, npad, a.shape[1]), F32).at[:, :T].set(a.reshape(DB, T, -1))

    vs_new = qkv_s[:, q_dim + kv_dim:]
    win_k0 = state_win_k[0].reshape(DB, WINDOW, kv_dim)
    win_v0 = state_win_v[0].reshape(DB, WINDOW, kv_dim)
    o_s = swa_sample(to_hr(qs_rot), win_k0, win_v0, pad_new(ks_rot), pad_new(vs_new), swa_sinks[0], T)
    o_s = from_hr(o_s)
    ao_p = matmul(o_p, swa_w_o[0])
    ao_s = matmul(o_s, swa_w_o[0])

    win_k_prompt = kp_rot.reshape(B, S, kvh, HEAD_DIM)[:, S - WINDOW:][None]
    win_v_prompt = qkv_p[:, q_dim + kv_dim:].reshape(B, S, kvh, HEAD_DIM)[:, S - WINDOW:][None]
    win_k_sample = jnp.concatenate([win_k0[:, T:], ks_rot.reshape(DB, T, kv_dim)], axis=1)
    win_v_sample = jnp.concatenate([win_v0[:, T:], vs_new.reshape(DB, T, kv_dim)], axis=1)
    win_k_sample = win_k_sample.reshape(1, DB, WINDOW, kvh, HEAD_DIM)
    win_v_sample = win_v_sample.reshape(1, DB, WINDOW, kvh, HEAD_DIM)

    xp, hp, _ = norm_mod(xp, norm_g[0, 1], mod_p(0, 3), mod_p(0, 4), y=ao_p.reshape(B, S, D), gate=mod_p(0, 2))
    xs, hs, _ = norm_mod(xs, norm_g[0, 1], mod_s(0, 3), mod_s(0, 4), y=ao_s.reshape(1, MS, D), gate=mod_s(0, 2),
                         h_dtype=F32)
    tmp = _tile(TILES["gu_tm"], MP)
    gu_p = gateup(hp.reshape(MP, D), ffn_w_g, ffn_w_u, zeros_te_p, tmp, TILES["gu_tn"])
    gu_s = gateup(hs.reshape(MS, D), ffn_w_g, ffn_w_u, zeros_te_s, MS, TILES["gu_tn"])
    f_p = down(gu_p, ffn_w_d, zeros_te_p, tmp, TILES["dn_tn"], TILES["dn_tk"])
    f_s = down(gu_s, ffn_w_d, zeros_te_s, MS, TILES["dn_tn"], TILES["dn_tk"])

    xp, hp, _ = norm_mod(xp, norm_g[1, 0], mod_p(1, 0), mod_p(1, 1), y=f_p.reshape(B, S, D), gate=mod_p(0, 5))
    xs, hs, _ = norm_mod(xs, norm_g[1, 0], mod_s(1, 0), mod_s(1, 1), y=f_s.reshape(1, MS, D), gate=mod_s(0, 5),
                         h_dtype=F32)
    qkv_dim = q_dim + 2 * kv_dim
    w_f = fox_w_qkvf[0][:, qkv_dim:]
    qkv_p = matmul(hp.reshape(MP, D), fox_w_qkvf[0], n_cols=qkv_dim)
    qkv_s = matmul(hs.reshape(MS, D), fox_w_qkvf[0], n_cols=qkv_dim)
    lf_p = logf_proj(hp.reshape(MP, D), w_f, fox_b_f[0])
    lf_s = logf_proj(hs.reshape(MS, D), w_f, fox_b_f[0])
    fcum = seq_cumsum(lf_p, B, S)[:, :n_heads].reshape(B, S, kvh, GROUP)
    o_p = fox_prompt(qkv_p, fcum.transpose(0, 2, 1, 3), fcum.transpose(0, 2, 3, 1), B, S, q_dim, kv_dim)

    q_hr = to_hr(qkv_s[:, :q_dim])
    eye = jnp.eye(kvh, dtype=F32)
    qbd = jnp.einsum("bhrd,hk->bhdkr", q_hr, eye).reshape(DB, kv_dim, kvh * GROUP * T).astype(BF16)
    ks_new = qkv_s[:, q_dim:q_dim + kv_dim]
    vs_new = qkv_s[:, q_dim + kv_dim:]
    lfs = lf_s[:, :n_heads]
    o_s = fox_sample(qbd, cache_k[0].reshape(-1, page, kv_dim), cache_v[0].reshape(-1, page, kv_dim),
                     cache_logf[0], page_table, pad_new(ks_new), pad_new(vs_new), pad_new(lfs), T)
    o_s = from_hr(o_s.reshape(DB, kvh, GROUP * T, HEAD_DIM))
    ao_p = matmul(o_p, fox_w_o[0])
    ao_s = matmul(o_s, fox_w_o[0])

    fox_k_prompt = qkv_p[:, q_dim:q_dim + kv_dim].reshape(1, B, S, kvh, HEAD_DIM)
    fox_v_prompt = qkv_p[:, q_dim + kv_dim:].reshape(1, B, S, kvh, HEAD_DIM)
    fox_logf_prompt = lf_p[:, :n_heads].reshape(1, B, S, n_heads)
    fox_k_sample = ks_new.reshape(1, DB, T, kvh, HEAD_DIM)
    fox_v_sample = vs_new.reshape(1, DB, T, kvh, HEAD_DIM)
    fox_logf_sample = lfs.reshape(1, DB, T, n_heads)

    router = (moe_w_router[0], moe_b_router[0])
    xp, hp, rp = norm_mod(xp, norm_g[1, 1], mod_p(1, 3), mod_p(1, 4), y=ao_p.reshape(B, S, D), gate=mod_p(1, 2),
                          router=router, h_dtype=F32)
    xs, hs, rs = norm_mod(xs, norm_g[1, 1], mod_s(1, 3), mod_s(1, 4), y=ao_s.reshape(1, MS, D), gate=mod_s(1, 2),
                          router=router, h_dtype=F32)
    h_all = jnp.concatenate([hp.reshape(MP, D), hs.reshape(MS, D)], axis=0)
    top_idx = jnp.concatenate([rp[0].reshape(MP, LANES)[:, :TOP_K], rs[0].reshape(MS, LANES)[:, :TOP_K]], axis=0)
    tm = TILES["moe_tm"]
    src_tok, dest, tile_expert = dispatch_plan(top_idx, n_experts, tm)
    a_sorted = gather_rows(h_all, src_tok)
    gu = gateup(a_sorted, moe_w_g[0], moe_w_u[0], tile_expert, tm, TILES["moe_gu_tn"])
    y_sorted = down(gu, moe_w_d[0], tile_expert, tm, TILES["moe_dn_tn"], TILES["moe_dn_tk"])
    y_prompt = combine(xp, y_sorted, dest[:MP], rp[1], mod_p(1, 5), final_g)
    y_sample = combine(xs, y_sorted, dest[MP:], rs[1], mod_s(1, 5), final_g).reshape(DB, T, D)

    return (y_prompt, y_sample, win_k_prompt, win_v_prompt, win_k_sample, win_v_sample,
            fox_k_prompt, fox_v_prompt, fox_logf_prompt, fox_k_sample, fox_v_sample, fox_logf_sample)
```

```python
import functools

import jax
import jax.numpy as jnp
from jax import lax
from jax.experimental import pallas as pl
from jax.experimental.pallas import tpu as pltpu

F32 = jnp.float32
BF16 = jnp.bfloat16
I32 = jnp.int32

HEAD_DIM = 128
GROUP = 4
ROT_DIM = HEAD_DIM // 4
ROPE_THETA = 500000.0
WINDOW = 128
RMS_EPS = 1e-6
TOP_K = 2
LANES = 128
NEG = -1e30
HIGHEST = lax.Precision.HIGHEST

VMEM_LIMIT = 56 * 1024 * 1024

TILES = dict(
    ada_tn=512,
    norm_ts=256,
    mm_tm=1024, mm_tn=512,
    gu_tm=1024, gu_tn=256,
    dn_tn=512, dn_tk=3584,
    moe_tm=512, moe_gu_tn=512, moe_dn_tn=512, moe_dn_tk=3584,
    fox_tq=256,
    fox_pages=8,
    gather_rows=256,
    combine_rows=128,
)


def _tile(pref, dim):
    t = min(pref, dim)
    assert dim % t == 0, (pref, dim)
    return t


def _params(sem):
    return pltpu.CompilerParams(dimension_semantics=sem, vmem_limit_bytes=VMEM_LIMIT)


def _sigmoid(x):
    return 1.0 / (1.0 + jnp.exp(-x))


def _ada_kernel(c_ref, w_ref, b_ref, o_ref):
    c = c_ref[...]
    o_ref[...] = jnp.dot(c * _sigmoid(c), w_ref[...], preferred_element_type=F32, precision=HIGHEST) + b_ref[...]


def ada_params(c_all, ada_w, ada_b):
    L, D, N = ada_w.shape
    R = c_all.shape[0]
    tn = _tile(TILES["ada_tn"], N)
    return pl.pallas_call(
        _ada_kernel,
        grid=(L, N // tn),
        in_specs=[
            pl.BlockSpec((R, D), lambda l, n: (0, 0)),
            pl.BlockSpec((None, D, tn), lambda l, n: (l, 0, n)),
            pl.BlockSpec((None, 1, tn), lambda l, n: (l, 0, n)),
        ],
        out_specs=pl.BlockSpec((None, R, tn), lambda l, n: (l, 0, n)),
        out_shape=jax.ShapeDtypeStruct((L, R, N), F32),
        compiler_params=_params(("arbitrary", "arbitrary")),
        name="ada_params",
    )(c_all, ada_w, ada_b.reshape(L, 1, N))


def _norm_mod_kernel(*refs, resid, router, n_experts):
    it = iter(refs)
    x_ref = next(it)
    if resid:
        y_ref, gate_ref = next(it), next(it)
    g_ref, shift_ref, scale_ref = next(it), next(it), next(it)
    if router:
        wr_ref, br_ref = next(it), next(it)
    if resid:
        xo_ref = next(it)
    h_ref = next(it)
    if router:
        idx_ref, gts_ref = next(it), next(it)

    x = x_ref[...]
    if resid:
        x = x + gate_ref[...] * y_ref[...]
        xo_ref[...] = x
    ms = jnp.mean(x * x, axis=-1, keepdims=True)
    yn = x * lax.rsqrt(ms + RMS_EPS) * g_ref[...]
    h = yn * (1.0 + scale_ref[...]) + shift_ref[...]
    h_ref[...] = h.astype(h_ref.dtype)
    if router:
        logits = jnp.dot(h, wr_ref[...], preferred_element_type=F32, precision=HIGHEST) + br_ref[...]
        lane = lax.broadcasted_iota(I32, logits.shape, 1)
        logits = jnp.where(lane < n_experts, logits, NEG)
        v1 = jnp.max(logits, axis=-1, keepdims=True)
        i1 = jnp.min(jnp.where(logits == v1, lane, LANES), axis=-1, keepdims=True)
        rest = jnp.where(lane == i1, NEG, logits)
        v2 = jnp.max(rest, axis=-1, keepdims=True)
        i2 = jnp.min(jnp.where(rest == v2, lane, LANES), axis=-1, keepdims=True)
        e = jnp.exp(v2 - v1)
        g1 = 1.0 / (1.0 + e)
        g2 = e / (1.0 + e)
        idx_ref[...] = jnp.where(lane == 0, i1, jnp.where(lane == 1, i2, 0))
        gts_ref[...] = jnp.where(lane == 0, g1, jnp.where(lane == 1, g2, 0.0))


def norm_mod(x, g, shift, scale, *, y=None, gate=None, router=None, h_dtype=BF16):
    G, S, D = x.shape
    ts = _tile(TILES["norm_ts"], S)
    resid = y is not None

    def pspec(p):
        if p.shape[1] == 1:
            return pl.BlockSpec((None, 1, D), lambda b, i: (b, 0, 0))
        return pl.BlockSpec((None, ts, D), lambda b, i: (b, i, 0))

    xspec = pl.BlockSpec((None, ts, D), lambda b, i: (b, i, 0))
    args, in_specs = [x], [xspec]
    if resid:
        args += [y, gate]
        in_specs += [xspec, pspec(gate)]
    args += [g.reshape(1, D), shift, scale]
    in_specs += [pl.BlockSpec((1, D), lambda b, i: (0, 0)), pspec(shift), pspec(scale)]
    n_experts = 0
    if router is not None:
        w_r, b_r = router
        n_experts = w_r.shape[1]
        wr = jnp.zeros((D, LANES), F32).at[:, :n_experts].set(w_r)
        br = jnp.zeros((1, LANES), F32).at[0, :n_experts].set(b_r)
        args += [wr, br]
        in_specs += [pl.BlockSpec((D, LANES), lambda b, i: (0, 0)),
                     pl.BlockSpec((1, LANES), lambda b, i: (0, 0))]
    out_shape, out_specs = [], []
    if resid:
        out_shape.append(jax.ShapeDtypeStruct((G, S, D), F32))
        out_specs.append(xspec)
    out_shape.append(jax.ShapeDtypeStruct((G, S, D), h_dtype))
    out_specs.append(xspec)
    if router is not None:
        lspec = pl.BlockSpec((None, ts, LANES), lambda b, i: (b, i, 0))
        out_shape += [jax.ShapeDtypeStruct((G, S, LANES), I32), jax.ShapeDtypeStruct((G, S, LANES), F32)]
        out_specs += [lspec, lspec]
    outs = pl.pallas_call(
        functools.partial(_norm_mod_kernel, resid=resid, router=router is not None, n_experts=n_experts),
        grid=(G, S // ts),
        in_specs=in_specs,
        out_specs=out_specs,
        out_shape=out_shape,
        compiler_params=_params(("arbitrary", "arbitrary")),
        name="norm_mod",
    )(*args)
    outs = list(outs)
    x_new = outs.pop(0) if resid else None
    h = outs.pop(0)
    return x_new, h, (tuple(outs) if router is not None else None)


def _wdot(a_ref, w_ref, wb, refresh):
    if a_ref.dtype == F32:
        return jnp.dot(a_ref[...], w_ref[...], preferred_element_type=F32, precision=HIGHEST)

    @pl.when(refresh)
    def _():
        wb[0][...] = w_ref[...].astype(BF16)

    return jnp.dot(a_ref[...], wb[0][...], preferred_element_type=F32)


def _wscratch(a, shape, n=1):
    return [] if a.dtype == F32 else [pltpu.VMEM(shape, BF16)] * n


def _mm_kernel(a_ref, w_ref, o_ref, *wb):
    o_ref[...] = _wdot(a_ref, w_ref, wb, pl.program_id(1) == 0).astype(o_ref.dtype)


def matmul(a, w, n_cols=None, out_dtype=F32):
    M, K = a.shape
    N = w.shape[-1] if n_cols is None else n_cols
    lead = w.ndim - 2
    tm = _tile(TILES["mm_tm"], M)
    tn = _tile(TILES["mm_tn"], N)
    return pl.pallas_call(
        _mm_kernel,
        grid=(N // tn, M // tm),
        in_specs=[
            pl.BlockSpec((tm, K), lambda n, m: (m, 0)),
            pl.BlockSpec((None,) * lead + (K, tn), lambda n, m: (0,) * lead + (0, n)),
        ],
        out_specs=pl.BlockSpec((tm, tn), lambda n, m: (m, n)),
        out_shape=jax.ShapeDtypeStruct((M, N), out_dtype),
        scratch_shapes=_wscratch(a, (K, tn)),
        compiler_params=_params(("arbitrary", "arbitrary")),
        name="matmul",
    )(a, w)


def _logf_kernel(a_ref, w_ref, b_ref, o_ref):
    if a_ref.dtype == F32:
        z = jnp.dot(a_ref[...], w_ref[...], preferred_element_type=F32, precision=HIGHEST)
    else:
        z = jnp.dot(a_ref[...], w_ref[...].astype(BF16), preferred_element_type=F32)
    z = z + b_ref[...]
    o_ref[...] = jnp.minimum(z, 0.0) - jnp.log(1.0 + jnp.exp(-jnp.abs(z)))


def logf_proj(a, w_f, b_f):
    M, K = a.shape
    nh = w_f.shape[1]
    wp = jnp.zeros((K, LANES), F32).at[:, :nh].set(w_f)
    bp = jnp.zeros((1, LANES), F32).at[0, :nh].set(b_f)
    tm = _tile(TILES["mm_tm"], M)
    return pl.pallas_call(
        _logf_kernel,
        grid=(M // tm,),
        in_specs=[
            pl.BlockSpec((tm, K), lambda m: (m, 0)),
            pl.BlockSpec((K, LANES), lambda m: (0, 0)),
            pl.BlockSpec((1, LANES), lambda m: (0, 0)),
        ],
        out_specs=pl.BlockSpec((tm, LANES), lambda m: (m, 0)),
        out_shape=jax.ShapeDtypeStruct((M, LANES), F32),
        compiler_params=_params(("arbitrary",)),
        name="logf_proj",
    )(a, wp, bp)


def _new_group(te_ref, m):
    prev = te_ref[jnp.maximum(m - 1, 0)]
    return jnp.logical_or(m == 0, te_ref[m] != prev)


def _tile_valid(te_ref, m):
    return te_ref[pl.num_programs(1) + m] > 0


def _expert_spec(w, rows, cols, index):
    lead = w.ndim - 3
    return pl.BlockSpec((None,) * (lead + 1) + (rows, cols),
                        lambda n, m, te: (0,) * lead + index(n, m, te))


def _gateup_kernel(te_ref, a_ref, wg_ref, wu_ref, o_ref, *wb):
    m = pl.program_id(1)
    tn = o_ref.shape[1]

    @pl.when(_tile_valid(te_ref, m))
    def _():
        if a_ref.dtype == F32:
            g = jnp.dot(a_ref[...], wg_ref[...], preferred_element_type=F32, precision=HIGHEST)
            u = jnp.dot(a_ref[...], wu_ref[...], preferred_element_type=F32, precision=HIGHEST)
        else:
            @pl.when(_new_group(te_ref, m))
            def _():
                wb[0][:, :tn] = wg_ref[...].astype(BF16)
                wb[0][:, tn:] = wu_ref[...].astype(BF16)

            gu = jnp.dot(a_ref[...], wb[0][...], preferred_element_type=F32)
            g, u = gu[:, :tn], gu[:, tn:]
        o_ref[...] = (g * _sigmoid(g) * u).astype(o_ref.dtype)

    @pl.when(jnp.logical_not(_tile_valid(te_ref, m)))
    def _():
        o_ref[...] = jnp.zeros(o_ref.shape, o_ref.dtype)


def gateup(a, w_g, w_u, tile_info, tm, tn):
    M, K = a.shape
    F = w_g.shape[-1]
    tn = _tile(tn, F)
    wspec = _expert_spec(w_g, K, tn, lambda n, m, te: (te[m], 0, n))
    return pl.pallas_call(
        _gateup_kernel,
        grid_spec=pltpu.PrefetchScalarGridSpec(
            num_scalar_prefetch=1,
            grid=(F // tn, M // tm),
            in_specs=[pl.BlockSpec((tm, K), lambda n, m, te: (m, 0)), wspec, wspec],
            out_specs=pl.BlockSpec((tm, tn), lambda n, m, te: (m, n)),
            scratch_shapes=_wscratch(a, (K, 2 * tn)),
        ),
        out_shape=jax.ShapeDtypeStruct((M, F), a.dtype),
        compiler_params=_params(("arbitrary", "arbitrary")),
        name="gateup",
    )(tile_info, a, w_g, w_u)


def _down_kernel(te_ref, a_ref, w_ref, *refs, has_partial):
    if has_partial:
        p_ref, o_ref, *wb = refs
    else:
        o_ref, *wb = refs
    m = pl.program_id(1)

    @pl.when(_tile_valid(te_ref, m))
    def _():
        acc = _wdot(a_ref, w_ref, wb, _new_group(te_ref, m))
        if has_partial:
            acc = p_ref[...] + acc
        o_ref[...] = acc

    @pl.when(jnp.logical_not(_tile_valid(te_ref, m)))
    def _():
        o_ref[...] = jnp.zeros(o_ref.shape, o_ref.dtype)


def down(a, w_d, tile_info, tm, tn, tk):
    M, F = a.shape
    D = w_d.shape[-1]
    tn = _tile(tn, D)
    tk = _tile(tk, F)
    ospec = pl.BlockSpec((tm, tn), lambda n, m, te: (m, n))
    y = None
    for k in range(F // tk):
        in_specs = [
            pl.BlockSpec((tm, tk), lambda n, m, te, k=k: (m, k)),
            _expert_spec(w_d, tk, tn, lambda n, m, te, k=k: (te[m], k, n)),
        ]
        args = [tile_info, a, w_d]
        if y is not None:
            in_specs.append(ospec)
            args.append(y)
        y = pl.pallas_call(
            functools.partial(_down_kernel, has_partial=y is not None),
            grid_spec=pltpu.PrefetchScalarGridSpec(
                num_scalar_prefetch=1,
                grid=(D // tn, M // tm),
                in_specs=in_specs,
                out_specs=ospec,
                scratch_shapes=_wscratch(a, (tk, tn)),
            ),
            out_shape=jax.ShapeDtypeStruct((M, D), F32),
            compiler_params=_params(("arbitrary", "arbitrary")),
            name="down",
        )(*args)
    return y


def rope_tables(pos):
    inv_freq = ROPE_THETA ** (-jnp.arange(0, ROT_DIM, 2, dtype=F32) / ROT_DIM)
    ang = pos.astype(F32)[:, None] * inv_freq[None, :]
    cos, sin = jnp.cos(ang), jnp.sin(ang)
    n = pos.shape[0]
    half = ROT_DIM // 2
    ct = jnp.concatenate([cos, cos, jnp.ones((n, HEAD_DIM - ROT_DIM), F32)], axis=1)
    s1 = jnp.concatenate([-sin, jnp.zeros((n, HEAD_DIM - half), F32)], axis=1)
    s2 = jnp.concatenate([jnp.zeros((n, half), F32), sin, jnp.zeros((n, HEAD_DIM - ROT_DIM), F32)], axis=1)
    return ct, s1, s2


def _rope_kernel(q_ref, k_ref, c_ref, s1_ref, s2_ref, qo_ref, ko_ref):
    c, s1, s2 = c_ref[...], s1_ref[...], s2_ref[...]
    half = ROT_DIM // 2

    def rot(x):
        return x * c + pltpu.roll(x, HEAD_DIM - half, 1) * s1 + pltpu.roll(x, half, 1) * s2

    for hd in range(q_ref.shape[1] // HEAD_DIM):
        sl = slice(hd * HEAD_DIM, (hd + 1) * HEAD_DIM)
        qo_ref[:, sl] = rot(q_ref[:, sl]).astype(qo_ref.dtype)
    for hd in range(k_ref.shape[1] // HEAD_DIM):
        sl = slice(hd * HEAD_DIM, (hd + 1) * HEAD_DIM)
        ko_ref[:, sl] = rot(k_ref[:, sl])


def rope(qkv, tables, q_dim, kv_dim, q_dtype):
    M = qkv.shape[0]
    S = tables[0].shape[0]
    ts = _tile(TILES["norm_ts"], S)
    ns = S // ts
    tspec = pl.BlockSpec((ts, HEAD_DIM), lambda i: (i % ns, 0))
    return pl.pallas_call(
        _rope_kernel,
        grid=(M // ts,),
        in_specs=[
            pl.BlockSpec((ts, q_dim), lambda i: (i, 0)),
            pl.BlockSpec((ts, kv_dim), lambda i: (i, q_dim // kv_dim)),
            tspec, tspec, tspec,
        ],
        out_specs=[pl.BlockSpec((ts, q_dim), lambda i: (i, 0)), pl.BlockSpec((ts, kv_dim), lambda i: (i, 0))],
        out_shape=[jax.ShapeDtypeStruct((M, q_dim), q_dtype), jax.ShapeDtypeStruct((M, kv_dim), F32)],
        compiler_params=_params(("arbitrary",)),
        name="rope",
    )(qkv, qkv, *tables)


def _sink_softmax(s, sink, dtype=BF16):
    m = jnp.maximum(jnp.max(s, axis=-1, keepdims=True), sink)
    p = jnp.exp(s - m)
    den = jnp.sum(p, axis=-1, keepdims=True) + jnp.exp(sink - m)
    return (p / den).astype(dtype)


def _swa_prompt_kernel(sink_ref, q_ref, kp_ref, kc_ref, vp_ref, vc_ref, o_ref):
    i = pl.program_id(1)
    h = pl.program_id(2)
    blk = WINDOW
    kb = jnp.concatenate([kp_ref[...], kc_ref[...]], axis=0).astype(BF16)
    vb = jnp.concatenate([vp_ref[...], vc_ref[...]], axis=0).astype(BF16)
    qs = jnp.concatenate([q_ref[:, g * HEAD_DIM:(g + 1) * HEAD_DIM] for g in range(GROUP)], axis=0)
    s = lax.dot_general(qs, kb, (((1,), (1,)), ((), ())), preferred_element_type=F32) * (HEAD_DIM ** -0.5)
    qi = lax.broadcasted_iota(I32, (blk, 2 * blk), 0)
    kj = lax.broadcasted_iota(I32, (blk, 2 * blk), 1)
    lo = jnp.where(i > 0, 0, blk)
    valid = jnp.logical_and(kj >= jnp.maximum(qi, lo), kj <= qi + blk)
    for g in range(GROUP):
        sg = jnp.where(valid, s[g * blk:(g + 1) * blk], NEG)
        p = _sink_softmax(sg, sink_ref[h * GROUP + g])
        o = jnp.dot(p, vb, preferred_element_type=F32)
        o_ref[:, g * HEAD_DIM:(g + 1) * HEAD_DIM] = o.astype(o_ref.dtype)


def swa_prompt(q_rot, k_rot, qkv, sinks, B, S):
    M, Q = q_rot.shape
    KV = k_rot.shape[1]
    kvh = KV // HEAD_DIM
    nb = S // WINDOW
    gq = GROUP * HEAD_DIM
    v_col0 = (Q + KV) // HEAD_DIM
    cur = lambda b, i, h: (b * nb + i, h)
    prev = lambda b, i, h: (b * nb + jnp.maximum(i - 1, 0), h)
    vcur = lambda b, i, h: (b * nb + i, v_col0 + h)
    vprev = lambda b, i, h: (b * nb + jnp.maximum(i - 1, 0), v_col0 + h)
    blk = (WINDOW, HEAD_DIM)
    return pl.pallas_call(
        _swa_prompt_kernel,
        grid=(B, nb, kvh),
        in_specs=[
            pl.BlockSpec(memory_space=pltpu.SMEM),
            pl.BlockSpec((WINDOW, gq), cur),
            pl.BlockSpec(blk, prev), pl.BlockSpec(blk, cur),
            pl.BlockSpec(blk, vprev), pl.BlockSpec(blk, vcur),
        ],
        out_specs=pl.BlockSpec((WINDOW, gq), cur),
        out_shape=jax.ShapeDtypeStruct((M, Q), BF16),
        compiler_params=_params(("arbitrary", "arbitrary", "arbitrary")),
        name="swa_prompt",
    )(sinks, q_rot, k_rot, k_rot, qkv, qkv)


def _swa_sample_kernel(sink_ref, q_ref, wk_ref, wv_ref, kn_ref, vn_ref, o_ref, *, n_new):
    kvh, rows, _ = q_ref.shape
    npad = kn_ref.shape[0]
    nk = WINDOW + npad
    r = lax.broadcasted_iota(I32, (rows, nk), 0)
    c = lax.broadcasted_iota(I32, (rows, nk), 1)
    t = r % n_new
    valid = jnp.logical_and(c >= t, c <= t + WINDOW)
    rg = lax.broadcasted_iota(I32, (rows, 1), 0) // n_new
    for h in range(kvh):
        sl = slice(h * HEAD_DIM, (h + 1) * HEAD_DIM)
        kb = jnp.concatenate([wk_ref[:, sl], kn_ref[:, sl]], axis=0)
        vb = jnp.concatenate([wv_ref[:, sl], vn_ref[:, sl]], axis=0)
        s = lax.dot_general(q_ref[h], kb, (((1,), (1,)), ((), ())),
                            preferred_element_type=F32, precision=HIGHEST) * (HEAD_DIM ** -0.5)
        s = jnp.where(valid, s, NEG)
        sink = jnp.zeros((rows, 1), F32)
        for g in range(GROUP):
            sink = jnp.where(rg == g, sink_ref[h * GROUP + g], sink)
        p = _sink_softmax(s, sink, F32)
        o_ref[h] = jnp.dot(p, vb, preferred_element_type=F32, precision=HIGHEST)


def swa_sample(q_hr, win_k, win_v, k_new, v_new, sinks, n_new):
    DB, kvh, rows, _ = q_hr.shape
    KV = win_k.shape[2]
    npad = k_new.shape[1]
    qspec = pl.BlockSpec((None, kvh, rows, HEAD_DIM), lambda b: (b, 0, 0, 0))
    wspec = pl.BlockSpec((None, WINDOW, KV), lambda b: (b, 0, 0))
    nspec = pl.BlockSpec((None, npad, KV), lambda b: (b, 0, 0))
    return pl.pallas_call(
        functools.partial(_swa_sample_kernel, n_new=n_new),
        grid=(DB,),
        in_specs=[pl.BlockSpec(memory_space=pltpu.SMEM), qspec, wspec, wspec, nspec, nspec],
        out_specs=qspec,
        out_shape=jax.ShapeDtypeStruct((DB, kvh, rows, HEAD_DIM), F32),
        compiler_params=_params(("arbitrary",)),
        name="swa_sample",
    )(sinks, q_hr, win_k, win_v, k_new, v_new)


def _cumsum_kernel(x_ref, o_ref):
    n = x_ref.shape[0] // LANES
    r = lax.broadcasted_iota(I32, (LANES, LANES), 0)
    c = lax.broadcasted_iota(I32, (LANES, LANES), 1)
    tri = jnp.where(c <= r, 1.0, 0.0).astype(F32)
    carry = jnp.zeros((1, LANES), F32)
    for b in range(n):
        xb = x_ref[b * LANES:(b + 1) * LANES, :]
        ob = jnp.dot(tri, xb, preferred_element_type=F32, precision=HIGHEST) + carry
        o_ref[b * LANES:(b + 1) * LANES, :] = ob
        carry = ob[LANES - 1:LANES, :]


def seq_cumsum(x, B, S):
    return pl.pallas_call(
        _cumsum_kernel,
        grid=(B,),
        in_specs=[pl.BlockSpec((S, LANES), lambda b: (b, 0))],
        out_specs=pl.BlockSpec((S, LANES), lambda b: (b, 0)),
        out_shape=jax.ShapeDtypeStruct(x.shape, F32),
        compiler_params=_params(("arbitrary",)),
        name="seq_cumsum",
    )(x)


def _fox_prompt_kernel(q_ref, k_ref, v_ref, fq_ref, fk_ref, o_ref, *, tq):
    qi = pl.program_id(2)
    scale = HEAD_DIM ** -0.5
    qs = jnp.concatenate([q_ref[:, g * HEAD_DIM:(g + 1) * HEAD_DIM] for g in range(GROUP)],
                         axis=0).astype(BF16)
    fq = jnp.concatenate([fq_ref[:, g:g + 1] for g in range(GROUP)], axis=0)
    rows = GROUP * tq

    def scores(j):
        start = pl.multiple_of(j * tq, tq)
        kb = k_ref[pl.ds(start, tq), :].astype(BF16)
        vb = v_ref[pl.ds(start, tq), :].astype(BF16)
        fk = fk_ref[:, pl.ds(start, tq)]
        fkb = jnp.concatenate([jnp.broadcast_to(fk[g:g + 1, :], (tq, tq)) for g in range(GROUP)], axis=0)
        s = lax.dot_general(qs, kb, (((1,), (1,)), ((), ())), preferred_element_type=F32) * scale
        return s + fq - fkb, vb

    def update(carry, s, vb):
        m, l, acc = carry
        m_new = jnp.maximum(m, jnp.max(s, axis=-1, keepdims=True))
        alpha = jnp.exp(m - m_new)
        p = jnp.exp(s - m_new)
        l = alpha * l + jnp.sum(p, axis=-1, keepdims=True)
        acc = alpha * acc + jnp.dot(p.astype(BF16), vb, preferred_element_type=F32)
        return m_new, l, acc

    def body(j, carry):
        s, vb = scores(j)
        return update(carry, s, vb)

    init = (jnp.full((rows, 1), NEG, F32), jnp.zeros((rows, 1), F32), jnp.zeros((rows, HEAD_DIM), F32))
    carry = lax.fori_loop(0, qi, body, init)
    s, vb = scores(qi)
    qp = lax.broadcasted_iota(I32, (rows, tq), 0) % tq
    kp = lax.broadcasted_iota(I32, (rows, tq), 1)
    s = jnp.where(kp <= qp, s, NEG)
    m, l, acc = update(carry, s, vb)
    o = acc / l
    for g in range(GROUP):
        o_ref[:, g * HEAD_DIM:(g + 1) * HEAD_DIM] = o[g * tq:(g + 1) * tq].astype(o_ref.dtype)


def fox_prompt(qkv, f_col, f_row, B, S, q_dim, kv_dim):
    M = qkv.shape[0]
    kvh = kv_dim // HEAD_DIM
    tq = _tile(TILES["fox_tq"], S)
    nq = S // tq
    gq = GROUP * HEAD_DIM
    k0 = q_dim // HEAD_DIM
    v0 = (q_dim + kv_dim) // HEAD_DIM
    return pl.pallas_call(
        functools.partial(_fox_prompt_kernel, tq=tq),
        grid=(B, kvh, nq),
        in_specs=[
            pl.BlockSpec((tq, gq), lambda b, h, i: (b * nq + i, h)),
            pl.BlockSpec((S, HEAD_DIM), lambda b, h, i: (b, k0 + h)),
            pl.BlockSpec((S, HEAD_DIM), lambda b, h, i: (b, v0 + h)),
            pl.BlockSpec((None, None, tq, GROUP), lambda b, h, i: (b, h, i, 0)),
            pl.BlockSpec((None, None, GROUP, S), lambda b, h, i: (b, h, 0, 0)),
        ],
        out_specs=pl.BlockSpec((tq, gq), lambda b, h, i: (b * nq + i, h)),
        out_shape=jax.ShapeDtypeStruct((M, q_dim), BF16),
        compiler_params=_params(("arbitrary", "arbitrary", "arbitrary")),
        name="fox_prompt",
    )(qkv, qkv, qkv, f_col, f_row)


def _fox_sample_kernel(pt_ref, qbd_ref, *refs, n_pages_step, n_new, n_heads):
    C = n_pages_step
    k_refs, v_refs, lf_refs = refs[:C], refs[C:2 * C], refs[2 * C:3 * C]
    kn_ref, vn_ref, lfn_ref, o_ref, kb_ref, vb_ref, sfx_ref, m_ref, l_ref, acc_ref, carry_ref = refs[3 * C:]
    c_idx = pl.program_id(1)
    n_steps = pl.num_programs(1)
    page = k_refs[0].shape[0]
    ncol = qbd_ref.shape[1]
    npad = kn_ref.shape[0]
    scale = HEAD_DIM ** -0.5
    kvh = qbd_ref.shape[0] // HEAD_DIM
    rows_h = ncol // kvh

    col = lax.broadcasted_iota(I32, (n_heads, ncol), 1)
    hg = lax.broadcasted_iota(I32, (n_heads, ncol), 0)
    expand = jnp.where(col // n_new == hg, 1.0, 0.0).astype(F32)

    @pl.when(c_idx == 0)
    def _():
        m_ref[...] = jnp.full(m_ref.shape, NEG, F32)
        l_ref[...] = jnp.zeros(l_ref.shape, F32)
        acc_ref[...] = jnp.zeros(acc_ref.shape, F32)
        carry_ref[...] = jnp.zeros(carry_ref.shape, F32)

    lfn = jnp.dot(lfn_ref[...], expand, preferred_element_type=F32, precision=HIGHEST)
    rr = lax.broadcasted_iota(I32, (npad, npad), 0)
    cc = lax.broadcasted_iota(I32, (npad, npad), 1)
    a_key = jnp.dot(jnp.where(cc <= rr, 1.0, 0.0).astype(F32), lfn,
                    preferred_element_type=F32, precision=HIGHEST)
    srow = lax.broadcasted_iota(I32, (npad, ncol), 0)
    tcol = lax.broadcasted_iota(I32, (npad, ncol), 1) % n_new
    a_row = jnp.sum(jnp.where(srow == tcol, a_key, 0.0), axis=0, keepdims=True)

    def col_of(row_vec):
        return jnp.transpose(jnp.broadcast_to(row_vec, (ncol, ncol)))[:, 0:1]

    def update(s, vb):
        m = m_ref[0:1, :]
        m_new = jnp.maximum(m, jnp.max(s, axis=0, keepdims=True))
        alpha = jnp.exp(m - m_new)
        p = jnp.exp(s - m_new)
        l_ref[0:1, :] = alpha * l_ref[0:1, :] + jnp.sum(p, axis=0, keepdims=True)
        pv = lax.dot_general(p.astype(BF16), vb, (((0,), (0,)), ((), ())), preferred_element_type=F32)
        acc_ref[...] = col_of(alpha) * acc_ref[...] + pv
        m_ref[0:1, :] = m_new

    ur = lax.broadcasted_iota(I32, (page, page), 0)
    uc = lax.broadcasted_iota(I32, (page, page), 1)
    upper = jnp.where(uc > ur, 1.0, 0.0).astype(F32)
    carry = carry_ref[0:1, :]
    for i in range(C):
        lfe = jnp.dot(lf_refs[i][...], expand, preferred_element_type=F32, precision=HIGHEST)
        sfx_ref[i * page:(i + 1) * page, :] = (
            jnp.dot(upper, lfe, preferred_element_type=F32, precision=HIGHEST) + carry)
        carry = carry + jnp.sum(lfe, axis=0, keepdims=True)
        kb_ref[i * page:(i + 1) * page, :] = k_refs[i][...].astype(BF16)
        vb_ref[i * page:(i + 1) * page, :] = v_refs[i][...].astype(BF16)
    carry_ref[0:1, :] = carry

    s = jnp.dot(kb_ref[...], qbd_ref[...], preferred_element_type=F32) * scale + a_row + sfx_ref[...]
    update(s, vb_ref[...])

    @pl.when(c_idx == n_steps - 1)
    def _():
        s2 = jnp.dot(kn_ref[...].astype(BF16), qbd_ref[...], preferred_element_type=F32) * scale + a_row - a_key
        s2 = jnp.where(srow <= tcol, s2, NEG)
        update(s2, vn_ref[...].astype(BF16))
        o_full = acc_ref[...] / col_of(l_ref[0:1, :])
        for h in range(kvh):
            o_ref[h * rows_h:(h + 1) * rows_h, :] = o_full[h * rows_h:(h + 1) * rows_h,
                                                           h * HEAD_DIM:(h + 1) * HEAD_DIM]


def fox_sample(qbd, cache_k, cache_v, cache_logf, page_table, k_new, v_new, lf_new, n_new):
    DB, KV, ncol = qbd.shape
    page = cache_k.shape[1]
    n_heads = cache_logf.shape[2]
    n_pages = page_table.shape[1]
    npad = k_new.shape[1]
    C = _tile(TILES["fox_pages"], n_pages)
    n_steps = n_pages // C

    def page_map(i):
        return lambda b, c, pt: (pt[b * n_pages + (n_pages - 1 - (c * C + i))], 0, 0)

    kspecs = [pl.BlockSpec((None, page, KV), page_map(i)) for i in range(C)]
    lspecs = [pl.BlockSpec((None, page, n_heads), page_map(i)) for i in range(C)]
    per_seq = lambda b, c, pt: (b, 0, 0)
    return pl.pallas_call(
        functools.partial(_fox_sample_kernel, n_pages_step=C, n_new=n_new, n_heads=n_heads),
        grid_spec=pltpu.PrefetchScalarGridSpec(
            num_scalar_prefetch=1,
            grid=(DB, n_steps),
            in_specs=[pl.BlockSpec((None, KV, ncol), per_seq)] + kspecs + kspecs + lspecs + [
                pl.BlockSpec((None, npad, KV), per_seq),
                pl.BlockSpec((None, npad, KV), per_seq),
                pl.BlockSpec((None, npad, n_heads), per_seq),
            ],
            out_specs=pl.BlockSpec((None, ncol, HEAD_DIM), per_seq),
            scratch_shapes=[
                pltpu.VMEM((C * page, KV), BF16), pltpu.VMEM((C * page, KV), BF16),
                pltpu.VMEM((C * page, ncol), F32),
                pltpu.VMEM((8, ncol), F32), pltpu.VMEM((8, ncol), F32),
                pltpu.VMEM((ncol, KV), F32), pltpu.VMEM((8, ncol), F32),
            ],
        ),
        out_shape=jax.ShapeDtypeStruct((DB, ncol, HEAD_DIM), F32),
        compiler_params=_params(("arbitrary", "arbitrary")),
        name="fox_sample",
    )(page_table.reshape(-1), qbd, *([cache_k] * C), *([cache_v] * C), *([cache_logf] * C),
      k_new, v_new, lf_new)


def _gather_kernel(idx_ref, x_hbm, o_ref, buf_ref, sem):
    rows = buf_ref.shape[0]
    base = pl.program_id(0) * rows

    def row_copy(r, src_row):
        return pltpu.make_async_copy(x_hbm.at[pl.ds(src_row, 1), :], buf_ref.at[pl.ds(r, 1), :], sem)

    def start(r, c):
        row_copy(r, idx_ref[base + r]).start()
        return c

    def wait(r, c):
        row_copy(r, 0).wait()
        return c

    lax.fori_loop(0, rows, start, 0)
    lax.fori_loop(0, rows, wait, 0)
    o_ref[...] = buf_ref[...].astype(o_ref.dtype)


def gather_rows(x, src_rows):
    D = x.shape[1]
    P = src_rows.shape[0]
    rows = _tile(TILES["gather_rows"], P)
    return pl.pallas_call(
        _gather_kernel,
        grid_spec=pltpu.PrefetchScalarGridSpec(
            num_scalar_prefetch=1,
            grid=(P // rows,),
            in_specs=[pl.BlockSpec(memory_space=pl.ANY)],
            out_specs=pl.BlockSpec((rows, D), lambda i, idx: (i, 0)),
            scratch_shapes=[pltpu.VMEM((rows, D), F32), pltpu.SemaphoreType.DMA(())],
        ),
        out_shape=jax.ShapeDtypeStruct((P, D), BF16),
        compiler_params=_params(("arbitrary",)),
        name="gather_rows",
    )(src_rows, x)


def _combine_kernel(d0_ref, d1_ref, x_ref, y_hbm, gts_ref, gate_ref, fg_ref, o_ref, b0_ref, b1_ref, sem0, sem1,
                    *, seq_rows):
    rows = b0_ref.shape[0]
    base = pl.program_id(0) * seq_rows + pl.program_id(1) * rows

    def row_copy(d_ref_row, buf, r, sem):
        return pltpu.make_async_copy(y_hbm.at[pl.ds(d_ref_row, 1), :], buf.at[pl.ds(r, 1), :], sem)

    def start(r, c):
        row_copy(d0_ref[base + r], b0_ref, r, sem0).start()
        row_copy(d1_ref[base + r], b1_ref, r, sem1).start()
        return c

    def wait(r, c):
        row_copy(0, b0_ref, r, sem0).wait()
        row_copy(0, b1_ref, r, sem1).wait()
        return c

    lax.fori_loop(0, rows, start, 0)
    lax.fori_loop(0, rows, wait, 0)
    g = gts_ref[...]
    f = g[:, 0:1] * b0_ref[...] + g[:, 1:2] * b1_ref[...]
    x = x_ref[...] + gate_ref[...] * f
    ms = jnp.mean(x * x, axis=-1, keepdims=True)
    o_ref[...] = x * lax.rsqrt(ms + RMS_EPS) * fg_ref[...]


def combine(x, y_sorted, dest, gates, gate, final_g):
    G, S, D = x.shape
    rows = _tile(TILES["combine_rows"], S)
    xspec = pl.BlockSpec((None, rows, D), lambda b, i, d0, d1: (b, i, 0))
    if gate.shape[1] == 1:
        gspec = pl.BlockSpec((None, 1, D), lambda b, i, d0, d1: (b, 0, 0))
    else:
        gspec = xspec
    return pl.pallas_call(
        functools.partial(_combine_kernel, seq_rows=S),
        grid_spec=pltpu.PrefetchScalarGridSpec(
            num_scalar_prefetch=2,
            grid=(G, S // rows),
            in_specs=[
                xspec,
                pl.BlockSpec(memory_space=pl.ANY),
                pl.BlockSpec((None, rows, LANES), lambda b, i, d0, d1: (b, i, 0)),
                gspec,
                pl.BlockSpec((1, D), lambda b, i, d0, d1: (0, 0)),
            ],
            out_specs=xspec,
            scratch_shapes=[pltpu.VMEM((rows, D), F32), pltpu.VMEM((rows, D), F32),
                            pltpu.SemaphoreType.DMA(()), pltpu.SemaphoreType.DMA(())],
        ),
        out_shape=jax.ShapeDtypeStruct((G, S, D), F32),
        compiler_params=_params(("arbitrary", "arbitrary")),
        name="combine",
    )(dest[:, 0], dest[:, 1], x, y_sorted, gates, gate, final_g.reshape(1, D))


def dispatch_plan(top_idx, n_experts, tm):
    T = top_idx.shape[0]
    nk = T * TOP_K
    n_tiles = -(-nk // tm) + n_experts
    P = n_tiles * tm
    flat_e = top_idx.reshape(nk)
    onehot = (flat_e[:, None] == jnp.arange(n_experts, dtype=I32)[None, :]).astype(I32)
    ranks = jnp.cumsum(onehot, axis=0) - onehot
    rank = jnp.sum(ranks * onehot, axis=1)
    counts = jnp.sum(onehot, axis=0)
    padded = (counts + tm - 1) // tm * tm
    pend = jnp.cumsum(padded)
    pstart = pend - padded
    dest = pstart[flat_e] + rank
    src_tok = jnp.zeros((P,), I32).at[dest].set(jnp.arange(nk, dtype=I32) // TOP_K)
    tile_start = jnp.arange(n_tiles, dtype=I32) * tm
    tile_expert = jnp.minimum(jnp.searchsorted(pend, tile_start, side="right"), n_experts - 1).astype(I32)
    tile_valid = (tile_start < pend[n_experts - 1]).astype(I32)
    return src_tok, dest.reshape(T, TOP_K), jnp.concatenate([tile_expert, tile_valid])


def kernel(x_prompt, x_sample, state_win_k, state_win_v, cache_k, cache_v, cache_logf, page_table, c_prompt, c_sample, ada_w, ada_b, norm_g, final_g, swa_w_qkv, swa_sinks, swa_w_o, fox_w_qkvf, fox_b_f, fox_w_o, ffn_w_g, ffn_w_u, ffn_w_d, moe_w_router, moe_b_router, moe_w_g, moe_w_u, moe_w_d):
    B, S, D = x_prompt.shape
    DB, T, _ = x_sample.shape
    n_heads = D // HEAD_DIM
    kvh = n_heads // GROUP
    q_dim, kv_dim = n_heads * HEAD_DIM, kvh * HEAD_DIM
    n_experts = moe_w_router.shape[2]
    n_pages = page_table.shape[1]
    page = cache_k.shape[2]
    past = n_pages * page
    MP, MS = B * S, DB * T
    npad = 16
    assert T <= npad and n_heads * T == LANES

    rpad = -(-(B + DB) // 16) * 16
    c_all = jnp.zeros((rpad, D), F32).at[:B].set(c_prompt).at[B:B + DB].set(c_sample)
    mods = ada_params(c_all, ada_w, ada_b)

    def mod_p(i, k):
        return mods[i, :B, k * D:(k + 1) * D].reshape(B, 1, D)

    def mod_s(i, k):
        return jnp.repeat(mods[i, B:B + DB, k * D:(k + 1) * D], T, axis=0).reshape(1, MS, D)

    xp = x_prompt
    xs = x_sample.reshape(1, MS, D)
    def one_expert(n_tiles):
        return jnp.concatenate([jnp.zeros((n_tiles,), I32), jnp.ones((n_tiles,), I32)])

    zeros_te_p = one_expert(MP // _tile(TILES["gu_tm"], MP))
    zeros_te_s = one_expert(1)

    _, hp, _ = norm_mod(xp, norm_g[0, 0], mod_p(0, 0), mod_p(0, 1))
    _, hs, _ = norm_mod(xs, norm_g[0, 0], mod_s(0, 0), mod_s(0, 1), h_dtype=F32)
    qkv_p = matmul(hp.reshape(MP, D), swa_w_qkv)
    qkv_s = matmul(hs.reshape(MS, D), swa_w_qkv)
    qp_rot, kp_rot = rope(qkv_p, rope_tables(jnp.arange(S)), q_dim, kv_dim, BF16)
    pos_s = past + (jnp.arange(MS) % T)
    qs_rot, ks_rot = rope(qkv_s, rope_tables(pos_s), q_dim, kv_dim, F32)
    o_p = swa_prompt(qp_rot, kp_rot, qkv_p, swa_sinks[0], B, S)

    def to_hr(q):
        return q.reshape(DB, T, kvh, GROUP, HEAD_DIM).transpose(0, 2, 3, 1, 4).reshape(DB, kvh, GROUP * T, HEAD_DIM)

    def from_hr(o):
        return o.reshape(DB, kvh, GROUP, T, HEAD_DIM).transpose(0, 3, 1, 2, 4).reshape(MS, q_dim)

    def pad_new(a):
        return jnp.zeros((DB, npad, a.shape[1]), F32).at[:, :T].set(a.reshape(DB, T, -1))

    vs_new = qkv_s[:, q_dim + kv_dim:]
    win_k0 = state_win_k[0].reshape(DB, WINDOW, kv_dim)
    win_v0 = state_win_v[0].reshape(DB, WINDOW, kv_dim)
    o_s = swa_sample(to_hr(qs_rot), win_k0, win_v0, pad_new(ks_rot), pad_new(vs_new), swa_sinks[0], T)
    o_s = from_hr(o_s)
    ao_p = matmul(o_p, swa_w_o)
    ao_s = matmul(o_s, swa_w_o)

    win_k_prompt = kp_rot.reshape(B, S, kvh, HEAD_DIM)[:, S - WINDOW:][None]
    win_v_prompt = qkv_p[:, q_dim + kv_dim:].reshape(B, S, kvh, HEAD_DIM)[:, S - WINDOW:][None]
    win_k_sample = jnp.concatenate([win_k0[:, T:], ks_rot.reshape(DB, T, kv_dim)], axis=1)
    win_v_sample = jnp.concatenate([win_v0[:, T:], vs_new.reshape(DB, T, kv_dim)], axis=1)
    win_k_sample = win_k_sample.reshape(1, DB, WINDOW, kvh, HEAD_DIM)
    win_v_sample = win_v_sample.reshape(1, DB, WINDOW, kvh, HEAD_DIM)

    xp, hp, _ = norm_mod(xp, norm_g[0, 1], mod_p(0, 3), mod_p(0, 4), y=ao_p.reshape(B, S, D), gate=mod_p(0, 2))
    xs, hs, _ = norm_mod(xs, norm_g[0, 1], mod_s(0, 3), mod_s(0, 4), y=ao_s.reshape(1, MS, D), gate=mod_s(0, 2),
                         h_dtype=F32)
    tmp = _tile(TILES["gu_tm"], MP)
    gu_p = gateup(hp.reshape(MP, D), ffn_w_g, ffn_w_u, zeros_te_p, tmp, TILES["gu_tn"])
    gu_s = gateup(hs.reshape(MS, D), ffn_w_g, ffn_w_u, zeros_te_s, MS, TILES["gu_tn"])
    f_p = down(gu_p, ffn_w_d, zeros_te_p, tmp, TILES["dn_tn"], TILES["dn_tk"])
    f_s = down(gu_s, ffn_w_d, zeros_te_s, MS, TILES["dn_tn"], TILES["dn_tk"])

    xp, hp, _ = norm_mod(xp, norm_g[1, 0], mod_p(1, 0), mod_p(1, 1), y=f_p.reshape(B, S, D), gate=mod_p(0, 5))
    xs, hs, _ = norm_mod(xs, norm_g[1, 0], mod_s(1, 0), mod_s(1, 1), y=f_s.reshape(1, MS, D), gate=mod_s(0, 5),
                         h_dtype=F32)
    qkv_dim = q_dim + 2 * kv_dim
    w_f = fox_w_qkvf[0][:, qkv_dim:]
    qkv_p = matmul(hp.reshape(MP, D), fox_w_qkvf, n_cols=qkv_dim)
    qkv_s = matmul(hs.reshape(MS, D), fox_w_qkvf, n_cols=qkv_dim)
    lf_p = logf_proj(hp.reshape(MP, D), w_f, fox_b_f[0])
    lf_s = logf_proj(hs.reshape(MS, D), w_f, fox_b_f[0])
    fcum = seq_cumsum(lf_p, B, S)[:, :n_heads].reshape(B, S, kvh, GROUP)
    o_p = fox_prompt(qkv_p, fcum.transpose(0, 2, 1, 3), fcum.transpose(0, 2, 3, 1), B, S, q_dim, kv_dim)

    q_hr = to_hr(qkv_s[:, :q_dim])
    eye = jnp.eye(kvh, dtype=F32)
    qbd = jnp.einsum("bhrd,hk->bhdkr", q_hr, eye).reshape(DB, kv_dim, kvh * GROUP * T).astype(BF16)
    ks_new = qkv_s[:, q_dim:q_dim + kv_dim]
    vs_new = qkv_s[:, q_dim + kv_dim:]
    lfs = lf_s[:, :n_heads]
    o_s = fox_sample(qbd, cache_k[0].reshape(-1, page, kv_dim), cache_v[0].reshape(-1, page, kv_dim),
                     cache_logf[0], page_table, pad_new(ks_new), pad_new(vs_new), pad_new(lfs), T)
    o_s = from_hr(o_s.reshape(DB, kvh, GROUP * T, HEAD_DIM))
    ao_p = matmul(o_p, fox_w_o)
    ao_s = matmul(o_s, fox_w_o)

    fox_k_prompt = qkv_p[:, q_dim:q_dim + kv_dim].reshape(1, B, S, kvh, HEAD_DIM)
    fox_v_prompt = qkv_p[:, q_dim + kv_dim:].reshape(1, B, S, kvh, HEAD_DIM)
    fox_logf_prompt = lf_p[:, :n_heads].reshape(1, B, S, n_heads)
    fox_k_sample = ks_new.reshape(1, DB, T, kvh, HEAD_DIM)
    fox_v_sample = vs_new.reshape(1, DB, T, kvh, HEAD_DIM)
    fox_logf_sample = lfs.reshape(1, DB, T, n_heads)

    router = (moe_w_router[0], moe_b_router[0])
    xp, hp, rp = norm_mod(xp, norm_g[1, 1], mod_p(1, 3), mod_p(1, 4), y=ao_p.reshape(B, S, D), gate=mod_p(1, 2),
                          router=router, h_dtype=F32)
    xs, hs, rs = norm_mod(xs, norm_g[1, 1], mod_s(1, 3), mod_s(1, 4), y=ao_s.reshape(1, MS, D), gate=mod_s(1, 2),
                          router=router, h_dtype=F32)
    h_all = jnp.concatenate([hp.reshape(MP, D), hs.reshape(MS, D)], axis=0)
    top_idx = jnp.concatenate([rp[0].reshape(MP, LANES)[:, :TOP_K], rs[0].reshape(MS, LANES)[:, :TOP_K]], axis=0)
    tm = TILES["moe_tm"]
    src_tok, dest, tile_info = dispatch_plan(top_idx, n_experts, tm)
    a_sorted = gather_rows(h_all, src_tok)
    gu = gateup(a_sorted, moe_w_g, moe_w_u, tile_info, tm, TILES["moe_gu_tn"])
    y_sorted = down(gu, moe_w_d, tile_info, tm, TILES["moe_dn_tn"], TILES["moe_dn_tk"])
    y_prompt = combine(xp, y_sorted, dest[:MP], rp[1], mod_p(1, 5), final_g)
    y_sample = combine(xs, y_sorted, dest[MP:], rs[1], mod_s(1, 5), final_g).reshape(DB, T, D)

    return (y_prompt, y_sample, win_k_prompt, win_v_prompt, win_k_sample, win_v_sample,
            fox_k_prompt, fox_v_prompt, fox_logf_prompt, fox_k_sample, fox_v_sample, fox_logf_sample)
```
